```python
import math
import jax, jax.numpy as jnp
from jax import lax
import numpy as np

D_MODEL = 1024
BATCH = 32
SEQ = 2048
DEPTH = 1

N_META = 16
HEAD_DIM = 64
N_Q_HEADS = D_MODEL // HEAD_DIM
N_KV_HEADS = N_Q_HEADS // 4
Q_PER_KV = N_Q_HEADS // N_KV_HEADS
WINDOW = 128
BLOCK = 128
ATTN_WIDTH = N_Q_HEADS * HEAD_DIM
KV_WIDTH = N_KV_HEADS * HEAD_DIM
SSM_GROUP = 16
SSM_WIDTH = D_MODEL // 2
SSM_GROUPS = SSM_WIDTH // SSM_GROUP
SSM_STATE = 64
D_FF = ((8 * D_MODEL // 3 + 255) // 256) * 256
NORM_EPS = 1e-6
NEG_INF = -1e30
SPLITS = [ATTN_WIDTH, ATTN_WIDTH + KV_WIDTH, ATTN_WIDTH + 2 * KV_WIDTH,
          ATTN_WIDTH + 2 * KV_WIDTH + SSM_WIDTH,
          ATTN_WIDTH + 2 * KV_WIDTH + SSM_WIDTH + D_MODEL]
IN_WIDTH = SPLITS[-1] + D_MODEL

kernel_name = "hybrid_swa_s5_gated_macaron"


def rmsnorm(x, g):
    xf = x.astype(jnp.float32)
    y = xf * lax.rsqrt(jnp.mean(xf * xf, axis=-1, keepdims=True) + NORM_EPS)
    return (y * g.astype(jnp.float32)).astype(x.dtype)


def swiglu(h, w1, w3, w2):
    return (jax.nn.silu(h @ w1) * (h @ w3)) @ w2


def sink_softmax(scores, mask, sink):
    scores = jnp.where(mask, scores, NEG_INF)
    sink_b = jnp.broadcast_to(sink[:, :, None, None], scores.shape[:-1] + (1,))
    p = jax.nn.softmax(jnp.concatenate([scores, sink_b], axis=-1), axis=-1)
    return p[..., :-1]


def sliding_window_attention(q, k, v, sinks):
    b, l = q.shape[0], q.shape[1]
    s = l - N_META
    nb = s // BLOCK
    q = (q * (HEAD_DIM ** -0.5)).reshape(b, l, N_KV_HEADS, Q_PER_KV, HEAD_DIM)
    k = k.reshape(b, l, N_KV_HEADS, HEAD_DIM)
    v = v.reshape(b, l, N_KV_HEADS, HEAD_DIM)
    sink = sinks.astype(jnp.float32).reshape(N_KV_HEADS, Q_PER_KV)
    q_meta, q_real = q[:, :N_META], q[:, N_META:]
    k_meta, k_real = k[:, :N_META], k[:, N_META:]
    v_meta, v_real = v[:, :N_META], v[:, N_META:]

    sc_m = jnp.einsum('bqkgd,bskd->bkgqs', q_meta, k_meta).astype(jnp.float32)
    mask_m = jnp.tril(jnp.ones((N_META, N_META), dtype=bool))
    p_m = sink_softmax(sc_m, mask_m, sink)
    out_m = jnp.einsum('bkgqs,bskd->bqkgd', p_m.astype(v.dtype), v_meta)

    qb = q_real.reshape(b, nb, BLOCK, N_KV_HEADS, Q_PER_KV, HEAD_DIM)
    kb = k_real.reshape(b, nb, BLOCK, N_KV_HEADS, HEAD_DIM)
    vb = v_real.reshape(b, nb, BLOCK, N_KV_HEADS, HEAD_DIM)
    k_band = jnp.concatenate([jnp.concatenate([jnp.zeros_like(kb[:, :1]), kb[:, :-1]], axis=1), kb], axis=2)
    v_band = jnp.concatenate([jnp.concatenate([jnp.zeros_like(vb[:, :1]), vb[:, :-1]], axis=1), vb], axis=2)
    qi = jnp.arange(BLOCK)[:, None]
    kj = jnp.arange(2 * BLOCK)[None, :]
    rel = BLOCK + qi - kj
    band_ok = (rel >= 0) & (rel < WINDOW)
    meta_ok = jnp.ones((BLOCK, N_META), dtype=bool)

    def block_fn(args):
        n, qn, kn, vn = args
        keys = jnp.concatenate([k_meta, kn], axis=1)
        vals = jnp.concatenate([v_meta, vn], axis=1)
        sc = jnp.einsum('bqkgd,bskd->bkgqs', qn, keys).astype(jnp.float32)
        valid = band_ok & ((n - 1) * BLOCK + kj >= 0)
        mask = jnp.concatenate([meta_ok, valid], axis=1)
        p = sink_softmax(sc, mask, sink)
        return jnp.einsum('bkgqs,bskd->bqkgd', p.astype(vals.dtype), vals)

    out_r = lax.map(block_fn, (jnp.arange(nb, dtype=jnp.int32), jnp.moveaxis(qb, 1, 0),
                               jnp.moveaxis(k_band, 1, 0), jnp.moveaxis(v_band, 1, 0)))
    out_r = jnp.moveaxis(out_r, 0, 1).reshape(b, s, ATTN_WIDTH)
    return jnp.concatenate([out_m.reshape(b, N_META, ATTN_WIDTH), out_r], axis=1)


def s5_ssm(u, a_re, a_im, log_step, b_re, b_im, c_re, c_im, d_skip):
    b, l, _ = u.shape
    uf = u.astype(jnp.float32).reshape(b, l, SSM_GROUPS, SSM_GROUP)
    ar, ai = a_re.astype(jnp.float32), a_im.astype(jnp.float32)
    step = jnp.exp(log_step.astype(jnp.float32))[:, None]
    mag = jnp.exp(ar * step)
    ang = ai * step
    lam_re, lam_im = mag * jnp.cos(ang), mag * jnp.sin(ang)
    den = ar * ar + ai * ai
    nr, ni = lam_re - 1.0, lam_im
    coef_re = (nr * ar + ni * ai) / den
    coef_im = (ni * ar - nr * ai) / den
    br, bi = b_re.astype(jnp.float32), b_im.astype(jnp.float32)
    bb_re = coef_re[..., None] * br - coef_im[..., None] * bi
    bb_im = coef_re[..., None] * bi + coef_im[..., None] * br
    bu_re = jnp.einsum('blgc,gnc->blgn', uf, bb_re)
    bu_im = jnp.einsum('blgc,gnc->blgn', uf, bb_im)
    la_re = jnp.broadcast_to(lam_re[None, None], (1, l, SSM_GROUPS, SSM_STATE))
    la_im = jnp.broadcast_to(lam_im[None, None], (1, l, SSM_GROUPS, SSM_STATE))

    def combine(e1, e2):
        a1r, a1i, b1r, b1i = e1
        a2r, a2i, b2r, b2i = e2
        return (a2r * a1r - a2i * a1i, a2r * a1i + a2i * a1r,
                a2r * b1r - a2i * b1i + b2r, a2r * b1i + a2i * b1r + b2i)

    _, _, x_re, x_im = lax.associative_scan(combine, (la_re, la_im, bu_re, bu_im), axis=1)
    y = (jnp.einsum('blgn,gcn->blgc', x_re, c_re.astype(jnp.float32))
         - jnp.einsum('blgn,gcn->blgc', x_im, c_im.astype(jnp.float32)))
    y = y + d_skip.astype(jnp.float32).reshape(SSM_GROUPS, SSM_GROUP) * uf
    return y.reshape(b, l, SSM_WIDTH).astype(u.dtype)


def setup_inputs(seed: int = 0) -> dict:
    key = jax.random.key(seed)
    ks = jax.random.split(key, 32)
    nrm = lambda k, shape, scale: jax.random.normal(k, shape, jnp.float32) * scale
    n_idx = jnp.arange(SSM_STATE, dtype=jnp.float32)
    return {
        "x": nrm(ks[0], (BATCH, SEQ, D_MODEL), 1.0),
        "meta_tokens": nrm(ks[1], (N_META, D_MODEL), 1.0),
        "ffn1_norm": 1.0 + nrm(ks[2], (DEPTH, D_MODEL), 0.02),
        "ffn1_w1": nrm(ks[3], (DEPTH, D_MODEL, D_FF), D_MODEL ** -0.5),
        "ffn1_w3": nrm(ks[4], (DEPTH, D_MODEL, D_FF), D_MODEL ** -0.5),
        "ffn1_w2": nrm(ks[5], (DEPTH, D_FF, D_MODEL), D_FF ** -0.5),
        "mix_norm": 1.0 + nrm(ks[6], (DEPTH, D_MODEL), 0.02),
        "w_in": nrm(ks[7], (DEPTH, D_MODEL, IN_WIDTH), D_MODEL ** -0.5),
        "attn_sinks": nrm(ks[8], (DEPTH, N_Q_HEADS), 0.5),
        "ssm_a_re": -0.5 + nrm(ks[9], (DEPTH, SSM_GROUPS, SSM_STATE), 0.01),
        "ssm_a_im": math.pi * n_idx[None, None, :] + nrm(ks[10], (DEPTH, SSM_GROUPS, SSM_STATE), 0.01),
        "ssm_log_step": jax.random.uniform(ks[11], (DEPTH, SSM_GROUPS), jnp.float32,
                                           math.log(0.001), math.log(0.1)),
        "ssm_b_re": nrm(ks[12], (DEPTH, SSM_GROUPS, SSM_STATE, SSM_GROUP), (2 * SSM_GROUP) ** -0.5),
        "ssm_b_im": nrm(ks[13], (DEPTH, SSM_GROUPS, SSM_STATE, SSM_GROUP), (2 * SSM_GROUP) ** -0.5),
        "ssm_c_re": nrm(ks[14], (DEPTH, SSM_GROUPS, SSM_GROUP, SSM_STATE), SSM_STATE ** -0.5),
        "ssm_c_im": nrm(ks[15], (DEPTH, SSM_GROUPS, SSM_GROUP, SSM_STATE), SSM_STATE ** -0.5),
        "ssm_d": nrm(ks[16], (DEPTH, SSM_WIDTH), 1.0),
        "ssm_glu_a": nrm(ks[17], (DEPTH, SSM_WIDTH, D_MODEL), SSM_WIDTH ** -0.5),
        "ssm_glu_b": nrm(ks[18], (DEPTH, SSM_WIDTH, D_MODEL), SSM_WIDTH ** -0.5),
        "w_out": nrm(ks[19], (DEPTH, D_MODEL, D_MODEL), D_MODEL ** -0.5),
        "ffn2_norm": 1.0 + nrm(ks[20], (DEPTH, D_MODEL), 0.02),
        "ffn2_w1": nrm(ks[21], (DEPTH, D_MODEL, D_FF), D_MODEL ** -0.5),
        "ffn2_w3": nrm(ks[22], (DEPTH, D_MODEL, D_FF), D_MODEL ** -0.5),
        "ffn2_w2": nrm(ks[23], (DEPTH, D_FF, D_MODEL), D_FF ** -0.5),
        "final_norm": 1.0 + nrm(ks[24], (D_MODEL,), 0.02),
    }


def reference(x, meta_tokens, ffn1_norm, ffn1_w1, ffn1_w3, ffn1_w2, mix_norm, w_in,
              attn_sinks, ssm_a_re, ssm_a_im, ssm_log_step, ssm_b_re, ssm_b_im,
              ssm_c_re, ssm_c_im, ssm_d, ssm_glu_a, ssm_glu_b, w_out,
              ffn2_norm, ffn2_w1, ffn2_w3, ffn2_w2, final_norm):
    b = x.shape[0]
    meta = jnp.broadcast_to(meta_tokens[None].astype(x.dtype), (b, N_META, D_MODEL))
    h = jnp.concatenate([meta, x], axis=1)
    for i in range(DEPTH):
        h = h + 0.5 * swiglu(rmsnorm(h, ffn1_norm[i]), ffn1_w1[i], ffn1_w3[i], ffn1_w2[i])
        hn = rmsnorm(h, mix_norm[i])
        q, k, v, u, g_attn, g_ssm = jnp.split(hn @ w_in[i], SPLITS, axis=-1)
        attn = sliding_window_attention(q, k, v, attn_sinks[i])
        y = s5_ssm(u, ssm_a_re[i], ssm_a_im[i], ssm_log_step[i], ssm_b_re[i], ssm_b_im[i],
                   ssm_c_re[i], ssm_c_im[i], ssm_d[i])
        y = jax.nn.gelu(y)
        ssm = (y @ ssm_glu_a[i]) * jax.nn.sigmoid(y @ ssm_glu_b[i])
        merged = jax.nn.sigmoid(g_attn) * attn + jax.nn.sigmoid(g_ssm) * ssm
        h = h + merged @ w_out[i]
        h = h + 0.5 * swiglu(rmsnorm(h, ffn2_norm[i]), ffn2_w1[i], ffn2_w3[i], ffn2_w2[i])
    return rmsnorm(h, final_norm)[:, N_META:]
```

```python
import functools
import math

import jax
import jax.numpy as jnp
from jax import lax
from jax.experimental import pallas as pl
from jax.experimental.pallas import tpu as pltpu

F32 = jnp.float32
BF16 = jnp.bfloat16

N_META = 16
HEAD_DIM = 64
Q_PER_KV = 4
WINDOW = 128
SSM_GROUP = 16
SSM_STATE = 64
NORM_EPS = 1e-6
NEG_INF = -1e30

V7X_VMEM_BYTES = 64 * 1024 * 1024
V7X_SUBLANES = 8
V7X_BF16_ROWS = 16
SSM_BATCH_ROWS = V7X_SUBLANES
PERM_ROWS = V7X_BF16_ROWS * SSM_BATCH_ROWS


def _const_spec(shape):
    nd = len(shape)
    return pl.BlockSpec(shape, lambda *_: (0,) * nd, pipeline_mode=pl.Buffered(1))


def _dot(a, b):
    return jnp.dot(a, b, preferred_element_type=F32)


def _dot_nt(a, b):
    return lax.dot_general(a, b, (((1,), (1,)), ((), ())), preferred_element_type=F32)


def _rms(x, g):
    return x * lax.rsqrt(jnp.mean(x * x, axis=-1, keepdims=True) + NORM_EPS) * g


def _sigmoid(x):
    return 1.0 / (1.0 + jnp.exp(-x))


def _swiglu(xn_bf16, w1_ref, w3_ref, w2_ref):
    a = _dot(xn_bf16, w1_ref[...])
    b = _dot(xn_bf16, w3_ref[...])
    act = (a * _sigmoid(a) * b).astype(BF16)
    return _dot(act, w2_ref[...])


def _ffn_in_kernel(x_ref, g1_ref, w1_ref, w3_ref, w2_ref, gm_ref, win_ref, h_ref, proj_ref):
    x = x_ref[...]
    h = x + 0.5 * _swiglu(_rms(x, g1_ref[...]).astype(BF16), w1_ref, w3_ref, w2_ref)
    h_ref[...] = h
    hn = _rms(h, gm_ref[...]).astype(BF16)
    proj_ref[...] = _dot(hn, win_ref[...]).astype(BF16)


def _ffn_in(x2d, g1, w1, w3, w2, gm, win, *, tm, vmem):
    rows, d = x2d.shape
    dff = w1.shape[1]
    width = win.shape[1]
    tm = min(tm, rows)
    return pl.pallas_call(
        _ffn_in_kernel,
        grid=(rows // tm,),
        in_specs=[
            pl.BlockSpec((tm, d), lambda i: (i, 0)),
            _const_spec((1, d)),
            _const_spec((d, dff)),
            _const_spec((d, dff)),
            _const_spec((dff, d)),
            _const_spec((1, d)),
            _const_spec((d, width)),
        ],
        out_specs=[
            pl.BlockSpec((tm, d), lambda i: (i, 0)),
            pl.BlockSpec((tm, width), lambda i: (i, 0)),
        ],
        out_shape=[
            jax.ShapeDtypeStruct((rows, d), F32),
            jax.ShapeDtypeStruct((rows, width), BF16),
        ],
        compiler_params=pltpu.CompilerParams(
            dimension_semantics=("parallel",), vmem_limit_bytes=vmem),
        name="ffn_in",
    )(x2d, g1, w1, w3, w2, gm, win)


def _ffn_out_kernel(h_ref, ag_ref, sg_ref, wout_ref, g2_ref, w1_ref, w3_ref, w2_ref, gf_ref, o_ref):
    merged = (ag_ref[...].astype(F32) + sg_ref[...].astype(F32)).astype(BF16)
    h = h_ref[...] + _dot(merged, wout_ref[...])
    h = h + 0.5 * _swiglu(_rms(h, g2_ref[...]).astype(BF16), w1_ref, w3_ref, w2_ref)
    o_ref[...] = _rms(h, gf_ref[...])


def _ffn_out(h1, ag, sg, wout, g2, w1, w3, w2, gf, *, tm, vmem):
    rows, d = h1.shape
    dff = w1.shape[1]
    tm = min(tm, rows)
    row_spec = pl.BlockSpec((tm, d), lambda i: (i, 0))
    return pl.pallas_call(
        _ffn_out_kernel,
        grid=(rows // tm,),
        in_specs=[
            row_spec, row_spec, row_spec,
            _const_spec((d, d)),
            _const_spec((1, d)),
            _const_spec((d, dff)),
            _const_spec((d, dff)),
            _const_spec((dff, d)),
            _const_spec((1, d)),
        ],
        out_specs=row_spec,
        out_shape=jax.ShapeDtypeStruct((rows, d), F32),
        compiler_params=pltpu.CompilerParams(
            dimension_semantics=("parallel",), vmem_limit_bytes=vmem),
        name="ffn_out",
    )(h1, ag, sg, wout, g2, w1, w3, w2, gf)


def _attn_kernel(sink_ref, q_ref, kvc_ref, kvp_ref, kvm_ref, gate_ref, o_ref, *, tq, n_kv):
    nsub = tq // WINDOW
    first_block = pl.program_id(1) * nsub
    kvw = n_kv * HEAD_DIM
    qi = lax.broadcasted_iota(jnp.int32, (WINDOW, 2 * WINDOW), 0)
    kj = lax.broadcasted_iota(jnp.int32, (WINDOW, 2 * WINDOW), 1)
    rel = WINDOW + qi - kj
    band_ok = (rel >= 0) & (rel < WINDOW)
    is_prev = kj < WINDOW
    scale = HEAD_DIM ** -0.5
    for j in range(nsub):
        rows = pl.ds(j * WINDOW, WINDOW)
        if j == 0:
            kv_prev = kvp_ref[...]
        else:
            kv_prev = kvc_ref[pl.ds((j - 1) * WINDOW, WINDOW), :]
        kv_cur = kvc_ref[rows, :]
        has_prev = (first_block + j) > 0
        mask = band_ok & (has_prev | jnp.logical_not(is_prev))
        mask4 = jnp.concatenate([mask] * Q_PER_KV, axis=0)
        head_out = []
        for kh in range(n_kv):
            ks = slice(kh * HEAD_DIM, (kh + 1) * HEAD_DIM)
            vs = slice(kvw + kh * HEAD_DIM, kvw + (kh + 1) * HEAD_DIM)
            k_band = jnp.concatenate([kv_prev[:, ks], kv_cur[:, ks]], axis=0)
            v_band = jnp.concatenate([kv_prev[:, vs], kv_cur[:, vs]], axis=0)
            k_meta = kvm_ref[:, ks]
            v_meta = kvm_ref[:, vs]
            q4 = jnp.concatenate(
                [q_ref[rows, pl.ds((kh * Q_PER_KV + g) * HEAD_DIM, HEAD_DIM)] for g in range(Q_PER_KV)],
                axis=0)
            sink = jnp.concatenate(
                [jnp.full((WINDOW, 1), sink_ref[kh * Q_PER_KV + g], F32) for g in range(Q_PER_KV)],
                axis=0)
            s = jnp.where(mask4, _dot_nt(q4, k_band) * scale, NEG_INF)
            sm = _dot_nt(q4, k_meta) * scale
            m = jnp.maximum(jnp.maximum(jnp.max(s, axis=-1, keepdims=True),
                                        jnp.max(sm, axis=-1, keepdims=True)), sink)
            p = jnp.exp(s - m)
            pm = jnp.exp(sm - m)
            denom = (jnp.sum(p, axis=-1, keepdims=True) + jnp.sum(pm, axis=-1, keepdims=True)
                     + jnp.exp(sink - m))
            o = _dot(p.astype(BF16), v_band) + _dot(pm.astype(BF16), v_meta)
            o = o * (1.0 / denom)
            for g in range(Q_PER_KV):
                head_out.append(o[g * WINDOW:(g + 1) * WINDOW, :])
        attn = jnp.concatenate(head_out, axis=-1)
        o_ref[rows, :] = (attn * _sigmoid(gate_ref[rows, :].astype(F32))).astype(BF16)


def _attention(sinks, proj, kv_meta, *, batch, seq, d, n_kv, tq, vmem):
    rows = batch * seq
    kvw2 = 2 * n_kv * HEAD_DIM
    tq = min(tq, seq)
    nsub = tq // WINDOW
    steps = seq // tq
    blocks = seq // WINDOW
    q_col = 0
    kv_col = d // kvw2
    gate_col = (d + kvw2 + d // 2) // d

    def prev_map(b, i):
        return (jnp.maximum(b * blocks + i * nsub - 1, 0), kv_col)

    return pl.pallas_call(
        functools.partial(_attn_kernel, tq=tq, n_kv=n_kv),
        grid=(batch, steps),
        in_specs=[
            pl.BlockSpec(memory_space=pltpu.SMEM),
            pl.BlockSpec((tq, d), lambda b, i: (b * steps + i, q_col)),
            pl.BlockSpec((tq, kvw2), lambda b, i: (b * steps + i, kv_col)),
            pl.BlockSpec((WINDOW, kvw2), prev_map),
            _const_spec((N_META, kvw2)),
            pl.BlockSpec((tq, d), lambda b, i: (b * steps + i, gate_col)),
        ],
        out_specs=pl.BlockSpec((tq, d), lambda b, i: (b * steps + i, 0)),
        out_shape=jax.ShapeDtypeStruct((rows, d), BF16),
        compiler_params=pltpu.CompilerParams(
            dimension_semantics=("parallel", "parallel"), vmem_limit_bytes=vmem),
        name="attention",
    )(sinks, proj, proj, proj, kv_meta, proj)


def _ssm_prep_kernel(ar_ref, ai_ref, ls_ref, br_ref, bi_ref, lr_ref, li_ref, bbr_ref, bbi_ref):
    ar = ar_ref[...]
    ai = ai_ref[...]
    step = jnp.exp(ls_ref[...])
    mag = jnp.exp(ar * step)
    ang = ai * step
    lam_re = mag * jnp.cos(ang)
    lam_im = mag * jnp.sin(ang)
    den = ar * ar + ai * ai
    nr = lam_re - 1.0
    ni = lam_im
    coef_re = (nr * ar + ni * ai) / den
    coef_im = (ni * ar - nr * ai) / den
    lr_ref[...] = lam_re
    li_ref[...] = lam_im
    br = br_ref[...]
    bi = bi_ref[...]
    cr = coef_re[:, None, :]
    ci = coef_im[:, None, :]
    bbr_ref[...] = cr * br - ci * bi
    bbi_ref[...] = cr * bi + ci * br


def _ssm_prep(a_re, a_im, log_step, b_re_t, b_im_t):
    g, n = a_re.shape
    c = b_re_t.shape[1]
    return pl.pallas_call(
        _ssm_prep_kernel,
        out_shape=[
            jax.ShapeDtypeStruct((g, n), F32),
            jax.ShapeDtypeStruct((g, n), F32),
            jax.ShapeDtypeStruct((g, c, n), F32),
            jax.ShapeDtypeStruct((g, c, n), F32),
        ],
        name="ssm_prep",
    )(a_re, a_im, log_step, b_re_t, b_im_t)


def _meta_state_kernel(u_ref, wb0_ref, wb1_ref, lr_ref, li_ref, x0_ref, *, half):
    u = u_ref[...]
    hw = u.shape[1] // 2
    bu0 = _dot(u[:, :hw], wb0_ref[...])
    bu1 = _dot(u[:, hw:], wb1_ref[...])
    bre = jnp.concatenate([bu0[:, :half], bu1[:, :half]], axis=-1)
    bim = jnp.concatenate([bu0[:, half:], bu1[:, half:]], axis=-1)
    lr = lr_ref[...]
    li = li_ref[...]
    xr = jnp.zeros_like(lr)
    xi = jnp.zeros_like(li)
    for t in range(u.shape[0]):
        nr = lr * xr - li * xi + bre[t:t + 1, :]
        ni = lr * xi + li * xr + bim[t:t + 1, :]
        xr, xi = nr, ni
    x0_ref[...] = jnp.concatenate([xr, xi], axis=-1)


def _meta_state(u_meta, wb0, wb1, lam_re_row, lam_im_row):
    nstate = lam_re_row.shape[1]
    return pl.pallas_call(
        functools.partial(_meta_state_kernel, half=nstate // 2),
        out_shape=jax.ShapeDtypeStruct((1, 2 * nstate), F32),
        name="meta_state",
    )(u_meta, wb0, wb1, lam_re_row, lam_im_row)


def _gelu_tanh(y):
    c = math.sqrt(2.0 / math.pi)
    return 0.5 * y * (1.0 + jnp.tanh(c * (y + 0.044715 * (y * y * y))))


def _ssm_kernel(u_ref, gate_ref, x0_ref, wb0_ref, wb1_ref, lr_ref, li_ref,
                wcr0_ref, wci0_ref, wcr1_ref, wci1_ref, dsk_ref, ga_ref, gb_ref,
                o_ref, bu_ref, up_ref, ybm_ref, st_ref, *, tc, nstate, lane_chunk):
    nb = SSM_BATCH_ROWS
    half = nstate // 2
    width = u_ref.shape[2]
    hw = width // 2
    ngrp = tc // V7X_BF16_ROWS

    @pl.when(pl.program_id(1) == 0)
    def _():
        st_ref[...] = jnp.broadcast_to(x0_ref[...], st_ref.shape)

    ri = lax.broadcasted_iota(jnp.int32, (PERM_ROWS, PERM_ROWS), 0)
    ci = lax.broadcasted_iota(jnp.int32, (PERM_ROWS, PERM_ROWS), 1)
    src_of_row = (ri % nb) * V7X_BF16_ROWS + ri // nb
    perm = (ci == src_of_row).astype(BF16)
    dst_of_row = (ri % V7X_BF16_ROWS) * nb + ri // V7X_BF16_ROWS
    perm_t = (ci == dst_of_row).astype(BF16)

    for k in range(ngrp):
        xk = jnp.concatenate(
            [u_ref[b, pl.ds(k * V7X_BF16_ROWS, V7X_BF16_ROWS), :] for b in range(nb)], axis=0)
        up_ref[pl.ds(k * PERM_ROWS, PERM_ROWS), :] = _dot(perm, xk).astype(BF16)

    up = up_ref[...]
    bu0 = _dot(up[:, :hw], wb0_ref[...])
    bu_ref[:, 0:half] = bu0[:, :half]
    bu_ref[:, nstate:nstate + half] = bu0[:, half:]
    bu1 = _dot(up[:, hw:], wb1_ref[...])
    bu_ref[:, half:nstate] = bu1[:, :half]
    bu_ref[:, nstate + half:] = bu1[:, half:]

    for c in range(nstate // lane_chunk):
        re_l = pl.ds(c * lane_chunk, lane_chunk)
        im_l = pl.ds(nstate + c * lane_chunk, lane_chunk)
        lr = jnp.broadcast_to(lr_ref[:, re_l], (nb, lane_chunk))
        li = jnp.broadcast_to(li_ref[:, re_l], (nb, lane_chunk))

        def step(t, carry, re_l=re_l, im_l=im_l, lr=lr, li=li):
            xr, xi = carry
            r = pl.ds(pl.multiple_of(t * nb, nb), nb)
            nr = lr * xr - li * xi + bu_ref[r, re_l]
            ni = lr * xi + li * xr + bu_ref[r, im_l]
            bu_ref[r, re_l] = nr
            bu_ref[r, im_l] = ni
            return nr, ni

        xr, xi = lax.fori_loop(0, tc, step, (st_ref[:, re_l], st_ref[:, im_l]), unroll=4)
        st_ref[:, re_l] = xr
        st_ref[:, im_l] = xi

    xs_re0 = bu_ref[:, 0:half].astype(BF16)
    xs_im0 = bu_ref[:, nstate:nstate + half].astype(BF16)
    y0 = _dot(xs_re0, wcr0_ref[...]) + _dot(xs_im0, wci0_ref[...])
    xs_re1 = bu_ref[:, half:nstate].astype(BF16)
    xs_im1 = bu_ref[:, nstate + half:].astype(BF16)
    y1 = _dot(xs_re1, wcr1_ref[...]) + _dot(xs_im1, wci1_ref[...])
    y = jnp.concatenate([y0, y1], axis=-1) + dsk_ref[...] * up_ref[...].astype(F32)
    yg = _gelu_tanh(y).astype(BF16)

    for k in range(ngrp):
        yk = _dot(perm_t, yg[k * PERM_ROWS:(k + 1) * PERM_ROWS, :]).astype(BF16)
        for b in range(nb):
            ybm_ref[pl.ds(b * tc + k * V7X_BF16_ROWS, V7X_BF16_ROWS), :] = (
                yk[b * V7X_BF16_ROWS:(b + 1) * V7X_BF16_ROWS, :])

    ybm = ybm_ref[...]
    ssm = _dot(ybm, ga_ref[...]) * _sigmoid(_dot(ybm, gb_ref[...]))
    d = o_ref.shape[2]
    gate = _sigmoid(gate_ref[...].astype(F32)).reshape(nb * tc, d)
    o_ref[...] = (ssm * gate).reshape(nb, tc, d).astype(BF16)


def _ssm(proj3, x0, wb0, wb1, lam_re_row, lam_im_row, wcr0, wci0, wcr1, wci1, dsk, ga, gb,
         *, d, tc, lane_chunk, vmem):
    batch, seq, total = proj3.shape
    nstate = lam_re_row.shape[1]
    width = dsk.shape[1]
    nb = SSM_BATCH_ROWS
    tc = min(tc, seq)
    u_col = (total - 2 * d - width) // width
    gate_col = (total - d) // d
    consts = [x0, wb0, wb1, lam_re_row, lam_im_row, wcr0, wci0, wcr1, wci1, dsk, ga, gb]
    return pl.pallas_call(
        functools.partial(_ssm_kernel, tc=tc, nstate=nstate, lane_chunk=lane_chunk),
        grid=(batch // nb, seq // tc),
        in_specs=[
            pl.BlockSpec((nb, tc, width), lambda g, t: (g, t, u_col)),
            pl.BlockSpec((nb, tc, d), lambda g, t: (g, t, gate_col)),
        ] + [_const_spec(c.shape) for c in consts],
        out_specs=pl.BlockSpec((nb, tc, d), lambda g, t: (g, t, 0)),
        out_shape=jax.ShapeDtypeStruct((batch, seq, d), BF16),
        scratch_shapes=[
            pltpu.VMEM((nb * tc, 2 * nstate), F32),
            pltpu.VMEM((nb * tc, width), BF16),
            pltpu.VMEM((nb * tc, width), BF16),
            pltpu.VMEM((nb, 2 * nstate), F32),
        ],
        compiler_params=pltpu.CompilerParams(
            dimension_semantics=("parallel", "arbitrary"), vmem_limit_bytes=vmem),
        name="ssm",
    )(proj3, proj3, *consts)


def _block_diag(blocks):
    g, r, c = blocks.shape
    eye = jnp.eye(g, dtype=blocks.dtype)
    return jnp.einsum("grc,gh->grhc", blocks, eye).reshape(g * r, g * c)


def kernel(x, meta_tokens, ffn1_norm, ffn1_w1, ffn1_w3, ffn1_w2, mix_norm, w_in, attn_sinks, ssm_a_re, ssm_a_im, ssm_log_step, ssm_b_re, ssm_b_im, ssm_c_re, ssm_c_im, ssm_d, ssm_glu_a, ssm_glu_b, w_out, ffn2_norm, ffn2_w1, ffn2_w3, ffn2_w2, final_norm):
    batch, seq, d = x.shape
    n_q = d // HEAD_DIM
    n_kv = n_q // Q_PER_KV
    kvw = n_kv * HEAD_DIM
    width = d // 2
    groups = width // SSM_GROUP
    nstate = groups * SSM_STATE
    half = nstate // 2
    hw = width // 2
    assert seq % WINDOW == 0 and batch % SSM_BATCH_ROWS == 0
    vmem = V7X_VMEM_BYTES - 8 * 1024 * 1024

    bf = lambda w: w.astype(BF16)
    row = lambda v: v.reshape(1, -1).astype(F32)

    lam_re, lam_im, bb_re_t, bb_im_t = _ssm_prep(
        ssm_a_re[0].astype(F32), ssm_a_im[0].astype(F32), ssm_log_step[0].astype(F32).reshape(groups, 1),
        jnp.swapaxes(ssm_b_re[0].astype(F32), 1, 2), jnp.swapaxes(ssm_b_im[0].astype(F32), 1, 2))
    bd_re = _block_diag(bb_re_t)
    bd_im = _block_diag(bb_im_t)
    wb0 = bf(jnp.concatenate([bd_re[:hw, :half], bd_im[:hw, :half]], axis=1))
    wb1 = bf(jnp.concatenate([bd_re[hw:, half:], bd_im[hw:, half:]], axis=1))
    cd_re = _block_diag(jnp.swapaxes(ssm_c_re[0].astype(F32), 1, 2))
    cd_im = _block_diag(jnp.swapaxes(-ssm_c_im[0].astype(F32), 1, 2))
    wcr0, wci0 = bf(cd_re[:half, :hw]), bf(cd_im[:half, :hw])
    wcr1, wci1 = bf(cd_re[half:, hw:]), bf(cd_im[half:, hw:])
    lam_re_row = lam_re.reshape(1, nstate)
    lam_im_row = lam_im.reshape(1, nstate)

    w1a, w3a, w2a = bf(ffn1_w1[0]), bf(ffn1_w3[0]), bf(ffn1_w2[0])
    win = bf(w_in[0])
    g1, gm = row(ffn1_norm[0]), row(mix_norm[0])

    _, proj_meta = _ffn_in(meta_tokens.astype(F32), g1, w1a, w3a, w2a, gm, win, tm=N_META, vmem=vmem)
    kv_meta = proj_meta[:, d:d + 2 * kvw]
    u_meta = proj_meta[:, d + 2 * kvw:d + 2 * kvw + width]
    x0 = _meta_state(u_meta, wb0, wb1, lam_re_row, lam_im_row)

    rows = batch * seq
    h1, proj = _ffn_in(x.reshape(rows, d), g1, w1a, w3a, w2a, gm, win, tm=256, vmem=vmem)

    attn_g = _attention(attn_sinks[0].astype(F32), proj, kv_meta,
                        batch=batch, seq=seq, d=d, n_kv=n_kv, tq=256, vmem=vmem)
    ssm_g = _ssm(proj.reshape(batch, seq, -1), x0, wb0, wb1, lam_re_row, lam_im_row,
                 wcr0, wci0, wcr1, wci1, row(ssm_d[0]), bf(ssm_glu_a[0]), bf(ssm_glu_b[0]),
                 d=d, tc=64, lane_chunk=512, vmem=vmem)

    out = _ffn_out(h1, attn_g, ssm_g.reshape(rows, d), bf(w_out[0]), row(ffn2_norm[0]),
                   bf(ffn2_w1[0]), bf(ffn2_w3[0]), bf(ffn2_w2[0]), row(final_norm), tm=256, vmem=vmem)
    return out.reshape(batch, seq, d)
```

```python
import functools
import math

import jax
import jax.numpy as jnp
from jax import lax
from jax.experimental import pallas as pl
from jax.experimental.pallas import tpu as pltpu

F32 = jnp.float32
BF16 = jnp.bfloat16

N_META = 16
HEAD_DIM = 64
Q_PER_KV = 4
WINDOW = 128
SSM_GROUP = 16
SSM_STATE = 64
NORM_EPS = 1e-6
NEG_INF = -1e30

V7X_VMEM_BYTES = 64 * 1024 * 1024
V7X_SUBLANES = 8
V7X_BF16_ROWS = 16
SSM_BATCH_ROWS = V7X_SUBLANES
PERM_ROWS = V7X_BF16_ROWS * SSM_BATCH_ROWS


def _const_spec(shape):
    nd = len(shape)
    return pl.BlockSpec(shape, lambda *_: (0,) * nd, pipeline_mode=pl.Buffered(1))


def _dot(a, b):
    return jnp.dot(a, b, preferred_element_type=F32)


def _dot_nt(a, b):
    return lax.dot_general(a, b, (((1,), (1,)), ((), ())), preferred_element_type=F32)


def _rms(x, g):
    return x * lax.rsqrt(jnp.mean(x * x, axis=-1, keepdims=True) + NORM_EPS) * g


def _sigmoid(x):
    return 0.5 * jnp.tanh(0.5 * x) + 0.5


def _swiglu(xn_bf16, w1_ref, w3_ref, w2_ref):
    a = _dot(xn_bf16, w1_ref[...])
    b = _dot(xn_bf16, w3_ref[...])
    act = (a * _sigmoid(a) * b).astype(BF16)
    return _dot(act, w2_ref[...])


def _ffn_in_kernel(x_ref, g1_ref, w1_ref, w3_ref, w2_ref, gm_ref, wrow_ref, wcol_ref,
                   h_ref, qt_ref, vt_ref, k_ref, u_ref, ga_ref, gs_ref):
    x = x_ref[...]
    h = x + 0.5 * _swiglu(_rms(x, g1_ref[...]).astype(BF16), w1_ref, w3_ref, w2_ref)
    h_ref[...] = h
    hn = _rms(h, gm_ref[...]).astype(BF16)
    pr = _dot(hn, wrow_ref[...])
    c0 = 0
    for ref in (k_ref, u_ref, ga_ref, gs_ref):
        c1 = c0 + ref.shape[1]
        ref[...] = pr[:, c0:c1].astype(BF16)
        c0 = c1
    pt = _dot_nt(wcol_ref[...], hn)
    nq = qt_ref.shape[0]
    qt_ref[...] = pt[:nq, :].astype(BF16)
    vt_ref[...] = pt[nq:, :].astype(BF16)


def _ffn_in(x2d, g1, w1, w3, w2, gm, wrow, wcol, *, d_kv, d_u, tm, vmem):
    rows, d = x2d.shape
    dff = w1.shape[1]
    tm = min(tm, rows)
    tok = lambda w: pl.BlockSpec((tm, w), lambda i: (i, 0))
    feat = lambda w: pl.BlockSpec((w, tm), lambda i: (0, i))
    return pl.pallas_call(
        _ffn_in_kernel,
        grid=(rows // tm,),
        in_specs=[
            tok(d),
            _const_spec((1, d)),
            _const_spec((d, dff)),
            _const_spec((d, dff)),
            _const_spec((dff, d)),
            _const_spec((1, d)),
            _const_spec(wrow.shape),
            _const_spec(wcol.shape),
        ],
        out_specs=[tok(d), feat(d), feat(d_kv), tok(d_kv), tok(d_u), tok(d), tok(d)],
        out_shape=[
            jax.ShapeDtypeStruct((rows, d), F32),
            jax.ShapeDtypeStruct((d, rows), BF16),
            jax.ShapeDtypeStruct((d_kv, rows), BF16),
            jax.ShapeDtypeStruct((rows, d_kv), BF16),
            jax.ShapeDtypeStruct((rows, d_u), BF16),
            jax.ShapeDtypeStruct((rows, d), BF16),
            jax.ShapeDtypeStruct((rows, d), BF16),
        ],
        compiler_params=pltpu.CompilerParams(
            dimension_semantics=("parallel",), vmem_limit_bytes=vmem),
        name="ffn_in",
    )(x2d, g1, w1, w3, w2, gm, wrow, wcol)


def _ffn_out_kernel(h_ref, ag_ref, sg_ref, wout_ref, g2_ref, w1_ref, w3_ref, w2_ref, gf_ref, o_ref):
    merged = (ag_ref[...].astype(F32) + sg_ref[...].astype(F32)).astype(BF16)
    h = h_ref[...] + _dot(merged, wout_ref[...])
    h = h + 0.5 * _swiglu(_rms(h, g2_ref[...]).astype(BF16), w1_ref, w3_ref, w2_ref)
    o_ref[...] = _rms(h, gf_ref[...])


def _ffn_out(h1, ag, sg, wout, g2, w1, w3, w2, gf, *, tm, vmem):
    rows, d = h1.shape
    dff = w1.shape[1]
    tm = min(tm, rows)
    row_spec = pl.BlockSpec((tm, d), lambda i: (i, 0))
    return pl.pallas_call(
        _ffn_out_kernel,
        grid=(rows // tm,),
        in_specs=[
            row_spec, row_spec, row_spec,
            _const_spec((d, d)),
            _const_spec((1, d)),
            _const_spec((d, dff)),
            _const_spec((d, dff)),
            _const_spec((dff, d)),
            _const_spec((1, d)),
        ],
        out_specs=row_spec,
        out_shape=jax.ShapeDtypeStruct((rows, d), F32),
        compiler_params=pltpu.CompilerParams(
            dimension_semantics=("parallel",), vmem_limit_bytes=vmem),
        name="ffn_out",
    )(h1, ag, sg, wout, g2, w1, w3, w2, gf)


def _attn_kernel(sink_ref, qt_ref, kc_ref, kp_ref, km_ref, vtc_ref, vtp_ref, vtm_ref, gate_ref, o_ref,
                 *, tq, n_kv):
    nsub = tq // WINDOW
    first_block = pl.program_id(1) * nsub
    lanes4 = Q_PER_KV * WINDOW
    kj = lax.broadcasted_iota(jnp.int32, (WINDOW, lanes4), 0)
    qi = lax.broadcasted_iota(jnp.int32, (WINDOW, lanes4), 1) % WINDOW
    cur_ok = kj <= qi
    pair_w = 2 * HEAD_DIM
    for j in range(nsub):
        cols = pl.ds(j * WINDOW, WINDOW)
        if j == 0:
            k_prev = kp_ref[...]
            vt_prev = vtp_ref[...]
        else:
            k_prev = kc_ref[pl.ds((j - 1) * WINDOW, WINDOW), :]
            vt_prev = vtc_ref[:, pl.ds((j - 1) * WINDOW, WINDOW)]
        k_all = jnp.concatenate([kc_ref[cols, :], k_prev, km_ref[...]], axis=0)
        vt_cur = vtc_ref[:, cols]
        prev_bias = jnp.where(first_block + j > 0, 0.0, NEG_INF).astype(F32)
        head_out = []
        for kh in range(n_kv):
            qt4 = jnp.concatenate(
                [qt_ref[pl.ds((kh * Q_PER_KV + g) * HEAD_DIM, HEAD_DIM), cols] for g in range(Q_PER_KV)],
                axis=1)
            zeros = jnp.zeros_like(qt4)
            w = jnp.concatenate([qt4, zeros] if kh % 2 == 0 else [zeros, qt4], axis=0)
            pair = kh // 2
            s_all = _dot(k_all[:, pair * pair_w:(pair + 1) * pair_w], w)
            s = jnp.where(cur_ok, s_all[:WINDOW], s_all[WINDOW:2 * WINDOW] + prev_bias)
            s_meta = s_all[2 * WINDOW:]
            sink = jnp.concatenate(
                [jnp.full((1, WINDOW), sink_ref[kh * Q_PER_KV + g], F32) for g in range(Q_PER_KV)],
                axis=1)
            m = jnp.maximum(jnp.maximum(jnp.max(s, axis=0, keepdims=True),
                                        jnp.max(s_meta, axis=0, keepdims=True)), sink)
            p = jnp.exp(s - m)
            pm = jnp.exp(s_meta - m)
            denom = (jnp.sum(p, axis=0, keepdims=True) + jnp.sum(pm, axis=0, keepdims=True)
                     + jnp.exp(sink - m))
            p_cur = jnp.where(cur_ok, p, 0.0).astype(BF16)
            p_prev = jnp.where(cur_ok, 0.0, p).astype(BF16)
            vs = pl.ds(kh * HEAD_DIM, HEAD_DIM)
            ot = (_dot(vt_cur[kh * HEAD_DIM:(kh + 1) * HEAD_DIM, :], p_cur)
                  + _dot(vt_prev[kh * HEAD_DIM:(kh + 1) * HEAD_DIM, :], p_prev)
                  + _dot(vtm_ref[vs, :], pm.astype(BF16)))
            ot = ot * (1.0 / denom)
            for g in range(Q_PER_KV):
                head_out.append(ot[:, g * WINDOW:(g + 1) * WINDOW])
        attn = jnp.concatenate(head_out, axis=0).T
        o_ref[cols, :] = (attn * _sigmoid(gate_ref[cols, :].astype(F32))).astype(BF16)


def _attention(sinks, qt, k, vt, k_meta, vt_meta, gate, *, batch, seq, n_kv, tq, vmem):
    d, rows = qt.shape
    kvw = n_kv * HEAD_DIM
    tq = min(tq, seq)
    nsub = tq // WINDOW
    steps = seq // tq
    blocks = seq // WINDOW

    def prev_block(b, i):
        return jnp.maximum(b * blocks + i * nsub - 1, 0)

    return pl.pallas_call(
        functools.partial(_attn_kernel, tq=tq, n_kv=n_kv),
        grid=(batch, steps),
        in_specs=[
            pl.BlockSpec(memory_space=pltpu.SMEM),
            pl.BlockSpec((d, tq), lambda b, i: (0, b * steps + i)),
            pl.BlockSpec((tq, kvw), lambda b, i: (b * steps + i, 0)),
            pl.BlockSpec((WINDOW, kvw), lambda b, i: (prev_block(b, i), 0)),
            _const_spec((N_META, kvw)),
            pl.BlockSpec((kvw, tq), lambda b, i: (0, b * steps + i)),
            pl.BlockSpec((kvw, WINDOW), lambda b, i: (0, prev_block(b, i))),
            _const_spec((kvw, N_META)),
            pl.BlockSpec((tq, d), lambda b, i: (b * steps + i, 0)),
        ],
        out_specs=pl.BlockSpec((tq, d), lambda b, i: (b * steps + i, 0)),
        out_shape=jax.ShapeDtypeStruct((rows, d), BF16),
        compiler_params=pltpu.CompilerParams(
            dimension_semantics=("parallel", "parallel"), vmem_limit_bytes=vmem),
        name="attention",
    )(sinks, qt, k, k, k_meta, vt, vt, vt_meta, gate)


def _ssm_prep_kernel(ar_ref, ai_ref, ls_ref, br_ref, bi_ref, lr_ref, li_ref, bbr_ref, bbi_ref):
    ar = ar_ref[...]
    ai = ai_ref[...]
    step = jnp.exp(ls_ref[...])
    mag = jnp.exp(ar * step)
    ang = ai * step
    lam_re = mag * jnp.cos(ang)
    lam_im = mag * jnp.sin(ang)
    den = ar * ar + ai * ai
    nr = lam_re - 1.0
    ni = lam_im
    coef_re = (nr * ar + ni * ai) / den
    coef_im = (ni * ar - nr * ai) / den
    lr_ref[...] = lam_re
    li_ref[...] = lam_im
    br = br_ref[...]
    bi = bi_ref[...]
    cr = coef_re[:, None, :]
    ci = coef_im[:, None, :]
    bbr_ref[...] = cr * br - ci * bi
    bbi_ref[...] = cr * bi + ci * br


def _ssm_prep(a_re, a_im, log_step, b_re_t, b_im_t):
    g, n = a_re.shape
    c = b_re_t.shape[1]
    return pl.pallas_call(
        _ssm_prep_kernel,
        out_shape=[
            jax.ShapeDtypeStruct((g, n), F32),
            jax.ShapeDtypeStruct((g, n), F32),
            jax.ShapeDtypeStruct((g, c, n), F32),
            jax.ShapeDtypeStruct((g, c, n), F32),
        ],
        name="ssm_prep",
    )(a_re, a_im, log_step, b_re_t, b_im_t)


def _meta_state_kernel(u_ref, wb0_ref, wb1_ref, lr_ref, li_ref, x0_ref, *, half):
    u = u_ref[...]
    hw = u.shape[1] // 2
    bu0 = _dot(u[:, :hw], wb0_ref[...])
    bu1 = _dot(u[:, hw:], wb1_ref[...])
    bre = jnp.concatenate([bu0[:, :half], bu1[:, :half]], axis=-1)
    bim = jnp.concatenate([bu0[:, half:], bu1[:, half:]], axis=-1)
    lr = lr_ref[...]
    li = li_ref[...]
    xr = jnp.zeros_like(lr)
    xi = jnp.zeros_like(li)
    for t in range(u.shape[0]):
        nr = lr * xr - li * xi + bre[t:t + 1, :]
        ni = lr * xi + li * xr + bim[t:t + 1, :]
        xr, xi = nr, ni
    x0_ref[...] = jnp.concatenate([xr, xi], axis=-1)


def _meta_state(u_meta, wb0, wb1, lam_re_row, lam_im_row):
    nstate = lam_re_row.shape[1]
    return pl.pallas_call(
        functools.partial(_meta_state_kernel, half=nstate // 2),
        out_shape=jax.ShapeDtypeStruct((1, 2 * nstate), F32),
        name="meta_state",
    )(u_meta, wb0, wb1, lam_re_row, lam_im_row)


def _gelu_tanh(y):
    c = math.sqrt(2.0 / math.pi)
    return 0.5 * y * (1.0 + jnp.tanh(c * (y + 0.044715 * (y * y * y))))


def _ssm_kernel(u_ref, gate_ref, x0_ref, wb0_ref, wb1_ref, lr_ref, li_ref,
                wcr0_ref, wci0_ref, wcr1_ref, wci1_ref, dsk_ref, ga_ref, gb_ref,
                o_ref, bu_ref, up_ref, ybm_ref, st_ref, *, tc, nstate, lane_chunk):
    nb = SSM_BATCH_ROWS
    half = nstate // 2
    width = u_ref.shape[2]
    hw = width // 2
    ngrp = tc // V7X_BF16_ROWS

    @pl.when(pl.program_id(1) == 0)
    def _():
        st_ref[...] = jnp.broadcast_to(x0_ref[...], st_ref.shape)

    ri = lax.broadcasted_iota(jnp.int32, (PERM_ROWS, PERM_ROWS), 0)
    ci = lax.broadcasted_iota(jnp.int32, (PERM_ROWS, PERM_ROWS), 1)
    src_of_row = (ri % nb) * V7X_BF16_ROWS + ri // nb
    perm = (ci == src_of_row).astype(BF16)
    dst_of_row = (ri % V7X_BF16_ROWS) * nb + ri // V7X_BF16_ROWS
    perm_t = (ci == dst_of_row).astype(BF16)

    for k in range(ngrp):
        xk = jnp.concatenate(
            [u_ref[b, pl.ds(k * V7X_BF16_ROWS, V7X_BF16_ROWS), :] for b in range(nb)], axis=0)
        up_ref[pl.ds(k * PERM_ROWS, PERM_ROWS), :] = _dot(perm, xk).astype(BF16)

    up = up_ref[...]
    bu0 = _dot(up[:, :hw], wb0_ref[...])
    bu_ref[:, 0:half] = bu0[:, :half]
    bu_ref[:, nstate:nstate + half] = bu0[:, half:]
    bu1 = _dot(up[:, hw:], wb1_ref[...])
    bu_ref[:, half:nstate] = bu1[:, :half]
    bu_ref[:, nstate + half:] = bu1[:, half:]

    for c in range(nstate // lane_chunk):
        re_l = pl.ds(c * lane_chunk, lane_chunk)
        im_l = pl.ds(nstate + c * lane_chunk, lane_chunk)
        lr = jnp.broadcast_to(lr_ref[:, re_l], (nb, lane_chunk))
        li = jnp.broadcast_to(li_ref[:, re_l], (nb, lane_chunk))

        def step(t, carry, re_l=re_l, im_l=im_l, lr=lr, li=li):
            xr, xi = carry
            r = pl.ds(pl.multiple_of(t * nb, nb), nb)
            nr = lr * xr - li * xi + bu_ref[r, re_l]
            ni = lr * xi + li * xr + bu_ref[r, im_l]
            bu_ref[r, re_l] = nr
            bu_ref[r, im_l] = ni
            return nr, ni

        xr, xi = lax.fori_loop(0, tc, step, (st_ref[:, re_l], st_ref[:, im_l]), unroll=4)
        st_ref[:, re_l] = xr
        st_ref[:, im_l] = xi

    xs_re0 = bu_ref[:, 0:half].astype(BF16)
    xs_im0 = bu_ref[:, nstate:nstate + half].astype(BF16)
    y0 = _dot(xs_re0, wcr0_ref[...]) + _dot(xs_im0, wci0_ref[...])
    xs_re1 = bu_ref[:, half:nstate].astype(BF16)
    xs_im1 = bu_ref[:, nstate + half:].astype(BF16)
    y1 = _dot(xs_re1, wcr1_ref[...]) + _dot(xs_im1, wci1_ref[...])
    y = jnp.concatenate([y0, y1], axis=-1) + dsk_ref[...] * up_ref[...].astype(F32)
    yg = _gelu_tanh(y).astype(BF16)

    for k in range(ngrp):
        yk = _dot(perm_t, yg[k * PERM_ROWS:(k + 1) * PERM_ROWS, :]).astype(BF16)
        for b in range(nb):
            ybm_ref[pl.ds(b * tc + k * V7X_BF16_ROWS, V7X_BF16_ROWS), :] = (
                yk[b * V7X_BF16_ROWS:(b + 1) * V7X_BF16_ROWS, :])

    ybm = ybm_ref[...]
    ssm = _dot(ybm, ga_ref[...]) * _sigmoid(_dot(ybm, gb_ref[...]))
    d = o_ref.shape[2]
    gate = _sigmoid(gate_ref[...].astype(F32)).reshape(nb * tc, d)
    o_ref[...] = (ssm * gate).reshape(nb, tc, d).astype(BF16)


def _ssm(u3, gate3, x0, wb0, wb1, lam_re_row, lam_im_row, wcr0, wci0, wcr1, wci1, dsk, ga, gb,
         *, tc, lane_chunk, vmem):
    batch, seq, width = u3.shape
    d = gate3.shape[2]
    nstate = lam_re_row.shape[1]
    nb = SSM_BATCH_ROWS
    tc = min(tc, seq)
    consts = [x0, wb0, wb1, lam_re_row, lam_im_row, wcr0, wci0, wcr1, wci1, dsk, ga, gb]
    return pl.pallas_call(
        functools.partial(_ssm_kernel, tc=tc, nstate=nstate, lane_chunk=lane_chunk),
        grid=(batch // nb, seq // tc),
        in_specs=[
            pl.BlockSpec((nb, tc, width), lambda g, t: (g, t, 0)),
            pl.BlockSpec((nb, tc, d), lambda g, t: (g, t, 0)),
        ] + [_const_spec(c.shape) for c in consts],
        out_specs=pl.BlockSpec((nb, tc, d), lambda g, t: (g, t, 0)),
        out_shape=jax.ShapeDtypeStruct((batch, seq, d), BF16),
        scratch_shapes=[
            pltpu.VMEM((nb * tc, 2 * nstate), F32),
            pltpu.VMEM((nb * tc, width), BF16),
            pltpu.VMEM((nb * tc, width), BF16),
            pltpu.VMEM((nb, 2 * nstate), F32),
        ],
        compiler_params=pltpu.CompilerParams(
            dimension_semantics=("parallel", "arbitrary"), vmem_limit_bytes=vmem),
        name="ssm",
    )(u3, gate3, *consts)


def _block_diag(blocks):
    g, r, c = blocks.shape
    eye = jnp.eye(g, dtype=blocks.dtype)
    return jnp.einsum("grc,gh->grhc", blocks, eye).reshape(g * r, g * c)


def kernel(x, meta_tokens, ffn1_norm, ffn1_w1, ffn1_w3, ffn1_w2, mix_norm, w_in, attn_sinks, ssm_a_re, ssm_a_im, ssm_log_step, ssm_b_re, ssm_b_im, ssm_c_re, ssm_c_im, ssm_d, ssm_glu_a, ssm_glu_b, w_out, ffn2_norm, ffn2_w1, ffn2_w3, ffn2_w2, final_norm):
    batch, seq, d = x.shape
    n_q = d // HEAD_DIM
    n_kv = n_q // Q_PER_KV
    kvw = n_kv * HEAD_DIM
    width = d // 2
    groups = width // SSM_GROUP
    nstate = groups * SSM_STATE
    half = nstate // 2
    hw = width // 2
    assert seq % WINDOW == 0 and batch % SSM_BATCH_ROWS == 0
    vmem = V7X_VMEM_BYTES - 8 * 1024 * 1024

    bf = lambda w: w.astype(BF16)
    row = lambda v: v.reshape(1, -1).astype(F32)

    lam_re, lam_im, bb_re_t, bb_im_t = _ssm_prep(
        ssm_a_re[0].astype(F32), ssm_a_im[0].astype(F32), ssm_log_step[0].astype(F32).reshape(groups, 1),
        jnp.swapaxes(ssm_b_re[0].astype(F32), 1, 2), jnp.swapaxes(ssm_b_im[0].astype(F32), 1, 2))
    bd_re = _block_diag(bb_re_t)
    bd_im = _block_diag(bb_im_t)
    wb0 = bf(jnp.concatenate([bd_re[:hw, :half], bd_im[:hw, :half]], axis=1))
    wb1 = bf(jnp.concatenate([bd_re[hw:, half:], bd_im[hw:, half:]], axis=1))
    cd_re = _block_diag(jnp.swapaxes(ssm_c_re[0].astype(F32), 1, 2))
    cd_im = _block_diag(jnp.swapaxes(-ssm_c_im[0].astype(F32), 1, 2))
    wcr0, wci0 = bf(cd_re[:half, :hw]), bf(cd_im[:half, :hw])
    wcr1, wci1 = bf(cd_re[half:, hw:]), bf(cd_im[half:, hw:])
    lam_re_row = lam_re.reshape(1, nstate)
    lam_im_row = lam_im.reshape(1, nstate)

    w1a, w3a, w2a = bf(ffn1_w1[0]), bf(ffn1_w3[0]), bf(ffn1_w2[0])
    g1, gm = row(ffn1_norm[0]), row(mix_norm[0])
    win = w_in[0]
    c_k, c_v, c_u = d, d + kvw, d + 2 * kvw
    wcol = bf(jnp.concatenate([win[:, :c_k] * (HEAD_DIM ** -0.5), win[:, c_v:c_u]], axis=1).T)
    wrow = bf(jnp.concatenate([win[:, c_k:c_v], win[:, c_u:]], axis=1))
    ffn_in = functools.partial(_ffn_in, g1=g1, w1=w1a, w3=w3a, w2=w2a, gm=gm, wrow=wrow, wcol=wcol,
                               d_kv=kvw, d_u=width, vmem=vmem)

    _, _, vt_meta, k_meta, u_meta, _, _ = ffn_in(meta_tokens.astype(F32), tm=N_META)
    x0 = _meta_state(u_meta, wb0, wb1, lam_re_row, lam_im_row)

    rows = batch * seq
    h1, qt, vt, k, u, gate_a, gate_s = ffn_in(x.reshape(rows, d), tm=256)

    attn_g = _attention(attn_sinks[0].astype(F32), qt, k, vt, k_meta, vt_meta, gate_a,
                        batch=batch, seq=seq, n_kv=n_kv, tq=256, vmem=vmem)
    ssm_g = _ssm(u.reshape(batch, seq, width), gate_s.reshape(batch, seq, d), x0, wb0, wb1,
                 lam_re_row, lam_im_row, wcr0, wci0, wcr1, wci1, row(ssm_d[0]),
                 bf(ssm_glu_a[0]), bf(ssm_glu_b[0]), tc=64, lane_chunk=512, vmem=vmem)

    out = _ffn_out(h1, attn_g, ssm_g.reshape(rows, d), bf(w_out[0]), row(ffn2_norm[0]),
                   bf(ffn2_w1[0]), bf(ffn2_w3[0]), bf(ffn2_w2[0]), row(final_norm), tm=256, vmem=vmem)
    return out.reshape(batch, seq, d)
```

```python
import functools
import math

import jax
import jax.numpy as jnp
from jax import lax
from jax.experimental import pallas as pl
from jax.experimental.pallas import tpu as pltpu

F32 = jnp.float32
BF16 = jnp.bfloat16

N_META = 16
HEAD_DIM = 64
Q_PER_KV = 4
WINDOW = 128
SSM_GROUP = 16
SSM_STATE = 64
NORM_EPS = 1e-6
NEG_INF = -1e30
LOG2E = math.log2(math.e)

V7X_VMEM_BYTES = 64 * 1024 * 1024
V7X_SUBLANES = 8
V7X_BF16_ROWS = 16
SSM_BATCH_ROWS = V7X_SUBLANES
PERM_ROWS = V7X_BF16_ROWS * SSM_BATCH_ROWS


def _const_spec(shape):
    nd = len(shape)
    return pl.BlockSpec(shape, lambda *_: (0,) * nd, pipeline_mode=pl.Buffered(1))


def _dot(a, b):
    return jnp.dot(a, b, preferred_element_type=F32)


def _dot_nt(a, b):
    return lax.dot_general(a, b, (((1,), (1,)), ((), ())), preferred_element_type=F32)


def _rms(x, g):
    return x * lax.rsqrt(jnp.mean(x * x, axis=-1, keepdims=True) + NORM_EPS) * g


def _sigmoid(x):
    return 0.5 * jnp.tanh(0.5 * x) + 0.5


def _swiglu(xn_bf16, w1_ref, w3_ref, w2_ref):
    a = _dot(xn_bf16, w1_ref[...])
    b = _dot(xn_bf16, w3_ref[...])
    act = (a * _sigmoid(a) * b).astype(BF16)
    return _dot(act, w2_ref[...])


def _ffn_in_kernel(x_ref, g1_ref, w1_ref, w3_ref, w2_ref, gm_ref, wrow_ref, wcol_ref,
                   h_ref, qt_ref, vt_ref, k_ref, u_ref, ga_ref, gs_ref):
    x = x_ref[...]
    h = x + 0.5 * _swiglu(_rms(x, g1_ref[...]).astype(BF16), w1_ref, w3_ref, w2_ref)
    h_ref[...] = h
    hn = _rms(h, gm_ref[...]).astype(BF16)
    pr = _dot(hn, wrow_ref[...])
    c0 = 0
    for ref in (k_ref, u_ref, ga_ref, gs_ref):
        c1 = c0 + ref.shape[1]
        ref[...] = pr[:, c0:c1].astype(BF16)
        c0 = c1
    pt = _dot_nt(wcol_ref[...], hn)
    nq = qt_ref.shape[0]
    qt_ref[...] = pt[:nq, :].astype(BF16)
    vt_ref[...] = pt[nq:, :].astype(BF16)


def _ffn_in(x2d, g1, w1, w3, w2, gm, wrow, wcol, *, d_kv, d_u, tm, vmem):
    rows, d = x2d.shape
    dff = w1.shape[1]
    tm = min(tm, rows)
    tok = lambda w: pl.BlockSpec((tm, w), lambda i: (i, 0))
    feat = lambda w: pl.BlockSpec((w, tm), lambda i: (0, i))
    return pl.pallas_call(
        _ffn_in_kernel,
        grid=(rows // tm,),
        in_specs=[
            tok(d),
            _const_spec((1, d)),
            _const_spec((d, dff)),
            _const_spec((d, dff)),
            _const_spec((dff, d)),
            _const_spec((1, d)),
            _const_spec(wrow.shape),
            _const_spec(wcol.shape),
        ],
        out_specs=[tok(d), feat(d), feat(d_kv), tok(d_kv), tok(d_u), tok(d), tok(d)],
        out_shape=[
            jax.ShapeDtypeStruct((rows, d), F32),
            jax.ShapeDtypeStruct((d, rows), BF16),
            jax.ShapeDtypeStruct((d_kv, rows), BF16),
            jax.ShapeDtypeStruct((rows, d_kv), BF16),
            jax.ShapeDtypeStruct((rows, d_u), BF16),
            jax.ShapeDtypeStruct((rows, d), BF16),
            jax.ShapeDtypeStruct((rows, d), BF16),
        ],
        compiler_params=pltpu.CompilerParams(
            dimension_semantics=("parallel",), vmem_limit_bytes=vmem),
        name="ffn_in",
    )(x2d, g1, w1, w3, w2, gm, wrow, wcol)


def _ffn_out_kernel(h_ref, ag_ref, sg_ref, wout_ref, g2_ref, w1_ref, w3_ref, w2_ref, gf_ref, o_ref):
    merged = (ag_ref[...].astype(F32) + sg_ref[...].astype(F32)).astype(BF16)
    h = h_ref[...] + _dot(merged, wout_ref[...])
    h = h + 0.5 * _swiglu(_rms(h, g2_ref[...]).astype(BF16), w1_ref, w3_ref, w2_ref)
    o_ref[...] = _rms(h, gf_ref[...])


def _ffn_out(h1, ag, sg, wout, g2, w1, w3, w2, gf, *, tm, vmem):
    rows, d = h1.shape
    dff = w1.shape[1]
    tm = min(tm, rows)
    row_spec = pl.BlockSpec((tm, d), lambda i: (i, 0))
    return pl.pallas_call(
        _ffn_out_kernel,
        grid=(rows // tm,),
        in_specs=[
            row_spec, row_spec, row_spec,
            _const_spec((d, d)),
            _const_spec((1, d)),
            _const_spec((d, dff)),
            _const_spec((d, dff)),
            _const_spec((dff, d)),
            _const_spec((1, d)),
        ],
        out_specs=row_spec,
        out_shape=jax.ShapeDtypeStruct((rows, d), F32),
        compiler_params=pltpu.CompilerParams(
            dimension_semantics=("parallel",), vmem_limit_bytes=vmem),
        name="ffn_out",
    )(h1, ag, sg, wout, g2, w1, w3, w2, gf)


def _attn_kernel(sink_ref, qt_ref, kc_ref, kp_ref, km_ref, vtc_ref, vtp_ref, vtm_ref, gate_ref, o_ref,
                 *, tq, n_kv):
    nsub = tq // WINDOW
    first_block = pl.program_id(1) * nsub
    lanes4 = Q_PER_KV * WINDOW
    kj = lax.broadcasted_iota(jnp.int32, (WINDOW, lanes4), 0)
    qi = lax.broadcasted_iota(jnp.int32, (WINDOW, lanes4), 1) % WINDOW
    cur_ok = kj <= qi
    cur_f = cur_ok.astype(F32)
    pair_w = 2 * HEAD_DIM

    def cols_of(j):
        return pl.ds(j * WINDOW, WINDOW)

    def scores(j, kh):
        if j == 0:
            k_prev = kp_ref[...]
        else:
            k_prev = kc_ref[cols_of(j - 1), :]
        k_all = jnp.concatenate([kc_ref[cols_of(j), :], k_prev, km_ref[...]], axis=0)
        qt4 = jnp.concatenate(
            [qt_ref[pl.ds((kh * Q_PER_KV + g) * HEAD_DIM, HEAD_DIM), cols_of(j)] for g in range(Q_PER_KV)],
            axis=1)
        zeros = jnp.zeros_like(qt4)
        w = jnp.concatenate([qt4, zeros] if kh % 2 == 0 else [zeros, qt4], axis=0)
        pair = kh // 2
        return _dot(k_all[:, pair * pair_w:(pair + 1) * pair_w], w)

    tasks = [(j, kh) for j in range(nsub) for kh in range(n_kv)]
    s_next = scores(*tasks[0])
    head_out = []
    for idx, (j, kh) in enumerate(tasks):
        s_all = s_next
        if idx + 1 < len(tasks):
            s_next = scores(*tasks[idx + 1])
        prev_bias = jnp.where(first_block + j > 0, 0.0, NEG_INF).astype(F32)
        s = jnp.where(cur_ok, s_all[:WINDOW], s_all[WINDOW:2 * WINDOW] + prev_bias)
        s_meta = s_all[2 * WINDOW:]
        sink = jnp.concatenate(
            [jnp.full((1, WINDOW), sink_ref[kh * Q_PER_KV + g] * LOG2E, F32) for g in range(Q_PER_KV)],
            axis=1)
        m = jnp.maximum(jnp.maximum(jnp.max(s, axis=0, keepdims=True),
                                    jnp.max(s_meta, axis=0, keepdims=True)), sink)
        p = jnp.exp2(s - m)
        pm = jnp.exp2(s_meta - m)
        denom = (jnp.sum(p, axis=0, keepdims=True) + jnp.sum(pm, axis=0, keepdims=True)
                 + jnp.exp2(sink - m))
        p_cur = p * cur_f
        p_prev = p - p_cur
        hs = slice(kh * HEAD_DIM, (kh + 1) * HEAD_DIM)
        if j == 0:
            vt_prev = vtp_ref[hs, :]
        else:
            vt_prev = vtc_ref[hs, cols_of(j - 1)]
        ot = (_dot(vtc_ref[hs, cols_of(j)], p_cur.astype(BF16))
              + _dot(vt_prev, p_prev.astype(BF16))
              + _dot(vtm_ref[hs, :], pm.astype(BF16)))
        ot = ot * (1.0 / denom)
        for g in range(Q_PER_KV):
            head_out.append(ot[:, g * WINDOW:(g + 1) * WINDOW])
        if kh == n_kv - 1:
            attn = jnp.concatenate(head_out, axis=0).T
            o_ref[cols_of(j), :] = (attn * _sigmoid(gate_ref[cols_of(j), :].astype(F32))).astype(BF16)
            head_out = []


def _attention(sinks, qt, k, vt, k_meta, vt_meta, gate, *, batch, seq, n_kv, tq, vmem):
    d, rows = qt.shape
    kvw = n_kv * HEAD_DIM
    tq = min(tq, seq)
    nsub = tq // WINDOW
    steps = seq // tq
    blocks = seq // WINDOW

    def prev_block(b, i):
        return jnp.maximum(b * blocks + i * nsub - 1, 0)

    return pl.pallas_call(
        functools.partial(_attn_kernel, tq=tq, n_kv=n_kv),
        grid=(batch, steps),
        in_specs=[
            pl.BlockSpec(memory_space=pltpu.SMEM),
            pl.BlockSpec((d, tq), lambda b, i: (0, b * steps + i)),
            pl.BlockSpec((tq, kvw), lambda b, i: (b * steps + i, 0)),
            pl.BlockSpec((WINDOW, kvw), lambda b, i: (prev_block(b, i), 0)),
            _const_spec((N_META, kvw)),
            pl.BlockSpec((kvw, tq), lambda b, i: (0, b * steps + i)),
            pl.BlockSpec((kvw, WINDOW), lambda b, i: (0, prev_block(b, i))),
            _const_spec((kvw, N_META)),
            pl.BlockSpec((tq, d), lambda b, i: (b * steps + i, 0)),
        ],
        out_specs=pl.BlockSpec((tq, d), lambda b, i: (b * steps + i, 0)),
        out_shape=jax.ShapeDtypeStruct((rows, d), BF16),
        compiler_params=pltpu.CompilerParams(
            dimension_semantics=("parallel", "parallel"), vmem_limit_bytes=vmem),
        name="attention",
    )(sinks, qt, k, k, k_meta, vt, vt, vt_meta, gate)


def _ssm_prep_kernel(ar_ref, ai_ref, ls_ref, br_ref, bi_ref, lr_ref, li_ref, bbr_ref, bbi_ref):
    ar = ar_ref[...]
    ai = ai_ref[...]
    step = jnp.exp(ls_ref[...])
    mag = jnp.exp(ar * step)
    ang = ai * step
    lam_re = mag * jnp.cos(ang)
    lam_im = mag * jnp.sin(ang)
    den = ar * ar + ai * ai
    nr = lam_re - 1.0
    ni = lam_im
    coef_re = (nr * ar + ni * ai) / den
    coef_im = (ni * ar - nr * ai) / den
    lr_ref[...] = lam_re
    li_ref[...] = lam_im
    br = br_ref[...]
    bi = bi_ref[...]
    cr = coef_re[:, None, :]
    ci = coef_im[:, None, :]
    bbr_ref[...] = cr * br - ci * bi
    bbi_ref[...] = cr * bi + ci * br


def _ssm_prep(a_re, a_im, log_step, b_re_t, b_im_t):
    g, n = a_re.shape
    c = b_re_t.shape[1]
    return pl.pallas_call(
        _ssm_prep_kernel,
        out_shape=[
            jax.ShapeDtypeStruct((g, n), F32),
            jax.ShapeDtypeStruct((g, n), F32),
            jax.ShapeDtypeStruct((g, c, n), F32),
            jax.ShapeDtypeStruct((g, c, n), F32),
        ],
        name="ssm_prep",
    )(a_re, a_im, log_step, b_re_t, b_im_t)


def _meta_state_kernel(u_ref, wb0_ref, wb1_ref, lr_ref, li_ref, x0_ref, *, half):
    u = u_ref[...]
    hw = u.shape[1] // 2
    bu0 = _dot(u[:, :hw], wb0_ref[...])
    bu1 = _dot(u[:, hw:], wb1_ref[...])
    bre = jnp.concatenate([bu0[:, :half], bu1[:, :half]], axis=-1)
    bim = jnp.concatenate([bu0[:, half:], bu1[:, half:]], axis=-1)
    lr = lr_ref[...]
    li = li_ref[...]
    xr = jnp.zeros_like(lr)
    xi = jnp.zeros_like(li)
    for t in range(u.shape[0]):
        nr = lr * xr - li * xi + bre[t:t + 1, :]
        ni = lr * xi + li * xr + bim[t:t + 1, :]
        xr, xi = nr, ni
    x0_ref[...] = jnp.concatenate([xr, xi], axis=-1)


def _meta_state(u_meta, wb0, wb1, lam_re_row, lam_im_row):
    nstate = lam_re_row.shape[1]
    return pl.pallas_call(
        functools.partial(_meta_state_kernel, half=nstate // 2),
        out_shape=jax.ShapeDtypeStruct((1, 2 * nstate), F32),
        name="meta_state",
    )(u_meta, wb0, wb1, lam_re_row, lam_im_row)


def _gelu_tanh(y):
    c = math.sqrt(2.0 / math.pi)
    return 0.5 * y * (1.0 + jnp.tanh(c * (y + 0.044715 * (y * y * y))))


def _ssm_kernel(u_ref, gate_ref, x0_ref, wb0_ref, wb1_ref, lr_ref, li_ref,
                wcr0_ref, wci0_ref, wcr1_ref, wci1_ref, dsk_ref, ga_ref, gb_ref,
                o_ref, bu_ref, up_ref, ybm_ref, st_ref, *, tc, nstate, lane_chunk):
    nb = SSM_BATCH_ROWS
    half = nstate // 2
    width = u_ref.shape[2]
    hw = width // 2
    ngrp = tc // V7X_BF16_ROWS

    @pl.when(pl.program_id(1) == 0)
    def _():
        st_ref[...] = jnp.broadcast_to(x0_ref[...], st_ref.shape)

    ri = lax.broadcasted_iota(jnp.int32, (PERM_ROWS, PERM_ROWS), 0)
    ci = lax.broadcasted_iota(jnp.int32, (PERM_ROWS, PERM_ROWS), 1)
    src_of_row = (ri % nb) * V7X_BF16_ROWS + ri // nb
    perm = (ci == src_of_row).astype(BF16)
    dst_of_row = (ri % V7X_BF16_ROWS) * nb + ri // V7X_BF16_ROWS
    perm_t = (ci == dst_of_row).astype(BF16)

    for k in range(ngrp):
        xk = jnp.concatenate(
            [u_ref[b, pl.ds(k * V7X_BF16_ROWS, V7X_BF16_ROWS), :] for b in range(nb)], axis=0)
        up_ref[pl.ds(k * PERM_ROWS, PERM_ROWS), :] = _dot(perm, xk).astype(BF16)

    up = up_ref[...]
    bu0 = _dot(up[:, :hw], wb0_ref[...])
    bu_ref[:, 0:half] = bu0[:, :half]
    bu_ref[:, nstate:nstate + half] = bu0[:, half:]
    bu1 = _dot(up[:, hw:], wb1_ref[...])
    bu_ref[:, half:nstate] = bu1[:, :half]
    bu_ref[:, nstate + half:] = bu1[:, half:]

    for c in range(nstate // lane_chunk):
        re_l = pl.ds(c * lane_chunk, lane_chunk)
        im_l = pl.ds(nstate + c * lane_chunk, lane_chunk)
        lr = jnp.broadcast_to(lr_ref[:, re_l], (nb, lane_chunk))
        li = jnp.broadcast_to(li_ref[:, re_l], (nb, lane_chunk))

        def step(t, carry, re_l=re_l, im_l=im_l, lr=lr, li=li):
            xr, xi = carry
            r = pl.ds(pl.multiple_of(t * nb, nb), nb)
            nr = lr * xr - li * xi + bu_ref[r, re_l]
            ni = lr * xi + li * xr + bu_ref[r, im_l]
            bu_ref[r, re_l] = nr
            bu_ref[r, im_l] = ni
            return nr, ni

        xr, xi = lax.fori_loop(0, tc, step, (st_ref[:, re_l], st_ref[:, im_l]), unroll=4)
        st_ref[:, re_l] = xr
        st_ref[:, im_l] = xi

    xs_re0 = bu_ref[:, 0:half].astype(BF16)
    xs_im0 = bu_ref[:, nstate:nstate + half].astype(BF16)
    y0 = _dot(xs_re0, wcr0_ref[...]) + _dot(xs_im0, wci0_ref[...])
    xs_re1 = bu_ref[:, half:nstate].astype(BF16)
    xs_im1 = bu_ref[:, nstate + half:].astype(BF16)
    y1 = _dot(xs_re1, wcr1_ref[...]) + _dot(xs_im1, wci1_ref[...])
    y = jnp.concatenate([y0, y1], axis=-1) + dsk_ref[...] * up_ref[...].astype(F32)
    yg = _gelu_tanh(y).astype(BF16)

    for k in range(ngrp):
        yk = _dot(perm_t, yg[k * PERM_ROWS:(k + 1) * PERM_ROWS, :]).astype(BF16)
        for b in range(nb):
            ybm_ref[pl.ds(b * tc + k * V7X_BF16_ROWS, V7X_BF16_ROWS), :] = (
                yk[b * V7X_BF16_ROWS:(b + 1) * V7X_BF16_ROWS, :])

    ybm = ybm_ref[...]
    ssm = _dot(ybm, ga_ref[...]) * _sigmoid(_dot(ybm, gb_ref[...]))
    d = o_ref.shape[2]
    gate = _sigmoid(gate_ref[...].astype(F32)).reshape(nb * tc, d)
    o_ref[...] = (ssm * gate).reshape(nb, tc, d).astype(BF16)


def _ssm(u3, gate3, x0, wb0, wb1, lam_re_row, lam_im_row, wcr0, wci0, wcr1, wci1, dsk, ga, gb,
         *, tc, lane_chunk, vmem):
    batch, seq, width = u3.shape
    d = gate3.shape[2]
    nstate = lam_re_row.shape[1]
    nb = SSM_BATCH_ROWS
    tc = min(tc, seq)
    consts = [x0, wb0, wb1, lam_re_row, lam_im_row, wcr0, wci0, wcr1, wci1, dsk, ga, gb]
    return pl.pallas_call(
        functools.partial(_ssm_kernel, tc=tc, nstate=nstate, lane_chunk=lane_chunk),
        grid=(batch // nb, seq // tc),
        in_specs=[
            pl.BlockSpec((nb, tc, width), lambda g, t: (g, t, 0)),
            pl.BlockSpec((nb, tc, d), lambda g, t: (g, t, 0)),
        ] + [_const_spec(c.shape) for c in consts],
        out_specs=pl.BlockSpec((nb, tc, d), lambda g, t: (g, t, 0)),
        out_shape=jax.ShapeDtypeStruct((batch, seq, d), BF16),
        scratch_shapes=[
            pltpu.VMEM((nb * tc, 2 * nstate), F32),
            pltpu.VMEM((nb * tc, width), BF16),
            pltpu.VMEM((nb * tc, width), BF16),
            pltpu.VMEM((nb, 2 * nstate), F32),
        ],
        compiler_params=pltpu.CompilerParams(
            dimension_semantics=("parallel", "arbitrary"), vmem_limit_bytes=vmem),
        name="ssm",
    )(u3, gate3, *consts)


def _block_diag(blocks):
    g, r, c = blocks.shape
    eye = jnp.eye(g, dtype=blocks.dtype)
    return jnp.einsum("grc,gh->grhc", blocks, eye).reshape(g * r, g * c)


def kernel(x, meta_tokens, ffn1_norm, ffn1_w1, ffn1_w3, ffn1_w2, mix_norm, w_in, attn_sinks, ssm_a_re, ssm_a_im, ssm_log_step, ssm_b_re, ssm_b_im, ssm_c_re, ssm_c_im, ssm_d, ssm_glu_a, ssm_glu_b, w_out, ffn2_norm, ffn2_w1, ffn2_w3, ffn2_w2, final_norm):
    batch, seq, d = x.shape
    n_q = d // HEAD_DIM
    n_kv = n_q // Q_PER_KV
    kvw = n_kv * HEAD_DIM
    width = d // 2
    groups = width // SSM_GROUP
    nstate = groups * SSM_STATE
    half = nstate // 2
    hw = width // 2
    assert seq % WINDOW == 0 and batch % SSM_BATCH_ROWS == 0
    vmem = V7X_VMEM_BYTES - 8 * 1024 * 1024

    bf = lambda w: w.astype(BF16)
    row = lambda v: v.reshape(1, -1).astype(F32)

    lam_re, lam_im, bb_re_t, bb_im_t = _ssm_prep(
        ssm_a_re[0].astype(F32), ssm_a_im[0].astype(F32), ssm_log_step[0].astype(F32).reshape(groups, 1),
        jnp.swapaxes(ssm_b_re[0].astype(F32), 1, 2), jnp.swapaxes(ssm_b_im[0].astype(F32), 1, 2))
    bd_re = _block_diag(bb_re_t)
    bd_im = _block_diag(bb_im_t)
    wb0 = bf(jnp.concatenate([bd_re[:hw, :half], bd_im[:hw, :half]], axis=1))
    wb1 = bf(jnp.concatenate([bd_re[hw:, half:], bd_im[hw:, half:]], axis=1))
    cd_re = _block_diag(jnp.swapaxes(ssm_c_re[0].astype(F32), 1, 2))
    cd_im = _block_diag(jnp.swapaxes(-ssm_c_im[0].astype(F32), 1, 2))
    wcr0, wci0 = bf(cd_re[:half, :hw]), bf(cd_im[:half, :hw])
    wcr1, wci1 = bf(cd_re[half:, hw:]), bf(cd_im[half:, hw:])
    lam_re_row = lam_re.reshape(1, nstate)
    lam_im_row = lam_im.reshape(1, nstate)

    w1a, w3a, w2a = bf(ffn1_w1[0]), bf(ffn1_w3[0]), bf(ffn1_w2[0])
    g1, gm = row(ffn1_norm[0]), row(mix_norm[0])
    win = w_in[0]
    c_k, c_v, c_u = d, d + kvw, d + 2 * kvw
    wcol = bf(jnp.concatenate([win[:, :c_k] * (HEAD_DIM ** -0.5 * LOG2E), win[:, c_v:c_u]], axis=1).T)
    wrow = bf(jnp.concatenate([win[:, c_k:c_v], win[:, c_u:]], axis=1))
    ffn_in = functools.partial(_ffn_in, g1=g1, w1=w1a, w3=w3a, w2=w2a, gm=gm, wrow=wrow, wcol=wcol,
                               d_kv=kvw, d_u=width, vmem=vmem)

    _, _, vt_meta, k_meta, u_meta, _, _ = ffn_in(meta_tokens.astype(F32), tm=N_META)
    x0 = _meta_state(u_meta, wb0, wb1, lam_re_row, lam_im_row)

    rows = batch * seq
    h1, qt, vt, k, u, gate_a, gate_s = ffn_in(x.reshape(rows, d), tm=256)

    attn_g = _attention(attn_sinks[0].astype(F32), qt, k, vt, k_meta, vt_meta, gate_a,
                        batch=batch, seq=seq, n_kv=n_kv, tq=512, vmem=vmem)
    ssm_g = _ssm(u.reshape(batch, seq, width), gate_s.reshape(batch, seq, d), x0, wb0, wb1,
                 lam_re_row, lam_im_row, wcr0, wci0, wcr1, wci1, row(ssm_d[0]),
                 bf(ssm_glu_a[0]), bf(ssm_glu_b[0]), tc=128, lane_chunk=512, vmem=vmem)

    out = _ffn_out(h1, attn_g, ssm_g.reshape(rows, d), bf(w_out[0]), row(ffn2_norm[0]),
                   bf(ffn2_w1[0]), bf(ffn2_w3[0]), bf(ffn2_w2[0]), row(final_norm), tm=512, vmem=vmem)
    return out.reshape(batch, seq, d)
```

```python
import functools
import math

import jax
import jax.numpy as jnp
from jax import lax
from jax.experimental import pallas as pl
from jax.experimental.pallas import tpu as pltpu

F32 = jnp.float32
BF16 = jnp.bfloat16

N_META = 16
HEAD_DIM = 64
Q_PER_KV = 4
WINDOW = 128
SSM_GROUP = 16
SSM_STATE = 64
NORM_EPS = 1e-6
NEG_INF = -1e30
LOG2E = math.log2(math.e)

V7X_VMEM_BYTES = 64 * 1024 * 1024
V7X_SUBLANES = 8
V7X_BF16_ROWS = 16
SSM_BATCH_ROWS = V7X_SUBLANES
PERM_ROWS = V7X_BF16_ROWS * SSM_BATCH_ROWS


def _const_spec(shape):
    nd = len(shape)
    return pl.BlockSpec(shape, lambda *_: (0,) * nd, pipeline_mode=pl.Buffered(1))


def _dot(a, b):
    return jnp.dot(a, b, preferred_element_type=F32)


def _dot_nt(a, b):
    return lax.dot_general(a, b, (((1,), (1,)), ((), ())), preferred_element_type=F32)


def _rms(x, g):
    return x * lax.rsqrt(jnp.mean(x * x, axis=-1, keepdims=True) + NORM_EPS) * g


def _sigmoid(x):
    return 0.5 * jnp.tanh(0.5 * x) + 0.5


def _swiglu(xn_bf16, w1_ref, w3_ref, w2_ref):
    a = _dot(xn_bf16, w1_ref[...])
    b = _dot(xn_bf16, w3_ref[...])
    act = (a * _sigmoid(a) * b).astype(BF16)
    return _dot(act, w2_ref[...])


def _ffn_in_kernel(x_ref, g1_ref, w1_ref, w3_ref, w2_ref, gm_ref, wrow_ref, wcol_ref,
                   h_ref, qt_ref, vt_ref, k_ref, u_ref, ga_ref, gs_ref):
    x = x_ref[...]
    h = x + 0.5 * _swiglu(_rms(x, g1_ref[...]).astype(BF16), w1_ref, w3_ref, w2_ref)
    h_ref[...] = h
    hn = _rms(h, gm_ref[...]).astype(BF16)
    pr = _dot(hn, wrow_ref[...])
    c0 = 0
    for ref, is_gate in ((k_ref, False), (u_ref, False), (ga_ref, True), (gs_ref, True)):
        c1 = c0 + ref.shape[1]
        piece = pr[:, c0:c1]
        ref[...] = (_sigmoid(piece) if is_gate else piece).astype(BF16)
        c0 = c1
    pt = _dot_nt(wcol_ref[...], hn)
    nq = qt_ref.shape[0]
    qt_ref[...] = pt[:nq, :].astype(BF16)
    vt_ref[...] = pt[nq:, :].astype(BF16)


def _ffn_in(x2d, g1, w1, w3, w2, gm, wrow, wcol, *, d_kv, d_u, tm, vmem):
    rows, d = x2d.shape
    dff = w1.shape[1]
    tm = min(tm, rows)
    tok = lambda w: pl.BlockSpec((tm, w), lambda i: (i, 0))
    feat = lambda w: pl.BlockSpec((w, tm), lambda i: (0, i))
    return pl.pallas_call(
        _ffn_in_kernel,
        grid=(rows // tm,),
        in_specs=[
            tok(d),
            _const_spec((1, d)),
            _const_spec((d, dff)),
            _const_spec((d, dff)),
            _const_spec((dff, d)),
            _const_spec((1, d)),
            _const_spec(wrow.shape),
            _const_spec(wcol.shape),
        ],
        out_specs=[tok(d), feat(d), feat(d_kv), tok(d_kv), tok(d_u), tok(d), tok(d)],
        out_shape=[
            jax.ShapeDtypeStruct((rows, d), F32),
            jax.ShapeDtypeStruct((d, rows), BF16),
            jax.ShapeDtypeStruct((d_kv, rows), BF16),
            jax.ShapeDtypeStruct((rows, d_kv), BF16),
            jax.ShapeDtypeStruct((rows, d_u), BF16),
            jax.ShapeDtypeStruct((rows, d), BF16),
            jax.ShapeDtypeStruct((rows, d), BF16),
        ],
        compiler_params=pltpu.CompilerParams(
            dimension_semantics=("parallel",), vmem_limit_bytes=vmem),
        name="ffn_in",
    )(x2d, g1, w1, w3, w2, gm, wrow, wcol)


def _ffn_out_kernel(h_ref, ag_ref, sg_ref, wout_ref, g2_ref, w1_ref, w3_ref, w2_ref, gf_ref, o_ref):
    merged = (ag_ref[...].astype(F32) + sg_ref[...].astype(F32)).astype(BF16)
    h = h_ref[...] + _dot(merged, wout_ref[...])
    h = h + 0.5 * _swiglu(_rms(h, g2_ref[...]).astype(BF16), w1_ref, w3_ref, w2_ref)
    o_ref[...] = _rms(h, gf_ref[...])


def _ffn_out(h1, ag, sg, wout, g2, w1, w3, w2, gf, *, tm, vmem):
    rows, d = h1.shape
    dff = w1.shape[1]
    tm = min(tm, rows)
    row_spec = pl.BlockSpec((tm, d), lambda i: (i, 0))
    return pl.pallas_call(
        _ffn_out_kernel,
        grid=(rows // tm,),
        in_specs=[
            row_spec, row_spec, row_spec,
            _const_spec((d, d)),
            _const_spec((1, d)),
            _const_spec((d, dff)),
            _const_spec((d, dff)),
            _const_spec((dff, d)),
            _const_spec((1, d)),
        ],
        out_specs=row_spec,
        out_shape=jax.ShapeDtypeStruct((rows, d), F32),
        compiler_params=pltpu.CompilerParams(
            dimension_semantics=("parallel",), vmem_limit_bytes=vmem),
        name="ffn_out",
    )(h1, ag, sg, wout, g2, w1, w3, w2, gf)


def _attn_kernel(sink_ref, qt_ref, kc_ref, kp_ref, km_ref, vtc_ref, vtp_ref, vtm_ref, gate_ref, o_ref,
                 *, tq, n_kv):
    nsub = tq // WINDOW
    first_block = pl.program_id(1) * nsub
    lanes4 = Q_PER_KV * WINDOW
    kj = lax.broadcasted_iota(jnp.int32, (WINDOW, lanes4), 0)
    qi = lax.broadcasted_iota(jnp.int32, (WINDOW, lanes4), 1) % WINDOW
    cur_ok = kj <= qi
    cur_f = cur_ok.astype(F32)
    pair_w = 2 * HEAD_DIM

    def cols_of(j):
        return pl.ds(j * WINDOW, WINDOW)

    def scores(j, kh):
        if j == 0:
            k_prev = kp_ref[...]
        else:
            k_prev = kc_ref[cols_of(j - 1), :]
        k_all = jnp.concatenate([kc_ref[cols_of(j), :], k_prev, km_ref[...]], axis=0)
        qt4 = jnp.concatenate(
            [qt_ref[pl.ds((kh * Q_PER_KV + g) * HEAD_DIM, HEAD_DIM), cols_of(j)] for g in range(Q_PER_KV)],
            axis=1)
        zeros = jnp.zeros_like(qt4)
        w = jnp.concatenate([qt4, zeros] if kh % 2 == 0 else [zeros, qt4], axis=0)
        pair = kh // 2
        return _dot(k_all[:, pair * pair_w:(pair + 1) * pair_w], w)

    tasks = [(j, kh) for j in range(nsub) for kh in range(n_kv)]
    s_next = scores(*tasks[0])
    head_out = []
    for idx, (j, kh) in enumerate(tasks):
        s_all = s_next
        if idx + 1 < len(tasks):
            s_next = scores(*tasks[idx + 1])
        s_prev = s_all[WINDOW:2 * WINDOW]
        if j == 0:
            s_prev = s_prev + jnp.where(first_block > 0, 0.0, NEG_INF).astype(F32)
        s = jnp.where(cur_ok, s_all[:WINDOW], s_prev)
        s_meta = s_all[2 * WINDOW:]
        sink = jnp.concatenate(
            [jnp.full((1, WINDOW), sink_ref[kh * Q_PER_KV + g] * LOG2E, F32) for g in range(Q_PER_KV)],
            axis=1)
        m = jnp.maximum(jnp.maximum(jnp.max(s, axis=0, keepdims=True),
                                    jnp.max(s_meta, axis=0, keepdims=True)), sink)
        p = jnp.exp2(s - m)
        pm = jnp.exp2(s_meta - m)
        denom = (jnp.sum(p, axis=0, keepdims=True) + jnp.sum(pm, axis=0, keepdims=True)
                 + jnp.exp2(sink - m))
        p_cur = p * cur_f
        p_prev = p - p_cur
        hs = slice(kh * HEAD_DIM, (kh + 1) * HEAD_DIM)
        if j == 0:
            vt_prev = vtp_ref[hs, :]
        else:
            vt_prev = vtc_ref[hs, cols_of(j - 1)]
        ot = (_dot(vtc_ref[hs, cols_of(j)], p_cur.astype(BF16))
              + _dot(vt_prev, p_prev.astype(BF16))
              + _dot(vtm_ref[hs, :], pm.astype(BF16)))
        ot = ot * (1.0 / denom)
        for g in range(Q_PER_KV):
            head_out.append(ot[:, g * WINDOW:(g + 1) * WINDOW])
        if kh == n_kv - 1:
            attn = jnp.concatenate(head_out, axis=0).T
            o_ref[cols_of(j), :] = (attn * gate_ref[cols_of(j), :].astype(F32)).astype(BF16)
            head_out = []


def _attention(sinks, qt, k, vt, k_meta, vt_meta, gate, *, batch, seq, n_kv, tq, vmem):
    d, rows = qt.shape
    kvw = n_kv * HEAD_DIM
    tq = min(tq, seq)
    nsub = tq // WINDOW
    steps = seq // tq
    blocks = seq // WINDOW

    def prev_block(b, i):
        return jnp.maximum(b * blocks + i * nsub - 1, 0)

    return pl.pallas_call(
        functools.partial(_attn_kernel, tq=tq, n_kv=n_kv),
        grid=(batch, steps),
        in_specs=[
            pl.BlockSpec(memory_space=pltpu.SMEM),
            pl.BlockSpec((d, tq), lambda b, i: (0, b * steps + i)),
            pl.BlockSpec((tq, kvw), lambda b, i: (b * steps + i, 0)),
            pl.BlockSpec((WINDOW, kvw), lambda b, i: (prev_block(b, i), 0)),
            _const_spec((N_META, kvw)),
            pl.BlockSpec((kvw, tq), lambda b, i: (0, b * steps + i)),
            pl.BlockSpec((kvw, WINDOW), lambda b, i: (0, prev_block(b, i))),
            _const_spec((kvw, N_META)),
            pl.BlockSpec((tq, d), lambda b, i: (b * steps + i, 0)),
        ],
        out_specs=pl.BlockSpec((tq, d), lambda b, i: (b * steps + i, 0)),
        out_shape=jax.ShapeDtypeStruct((rows, d), BF16),
        compiler_params=pltpu.CompilerParams(
            dimension_semantics=("parallel", "parallel"), vmem_limit_bytes=vmem),
        name="attention",
    )(sinks, qt, k, k, k_meta, vt, vt, vt_meta, gate)


def _ssm_prep_kernel(ar_ref, ai_ref, ls_ref, br_ref, bi_ref, lr_ref, li_ref, bbr_ref, bbi_ref):
    ar = ar_ref[...]
    ai = ai_ref[...]
    step = jnp.exp(ls_ref[...])
    mag = jnp.exp(ar * step)
    ang = ai * step
    lam_re = mag * jnp.cos(ang)
    lam_im = mag * jnp.sin(ang)
    den = ar * ar + ai * ai
    nr = lam_re - 1.0
    ni = lam_im
    coef_re = (nr * ar + ni * ai) / den
    coef_im = (ni * ar - nr * ai) / den
    lr_ref[...] = lam_re
    li_ref[...] = lam_im
    br = br_ref[...]
    bi = bi_ref[...]
    cr = coef_re[:, None, :]
    ci = coef_im[:, None, :]
    bbr_ref[...] = cr * br - ci * bi
    bbi_ref[...] = cr * bi + ci * br


def _ssm_prep(a_re, a_im, log_step, b_re_t, b_im_t):
    g, n = a_re.shape
    c = b_re_t.shape[1]
    return pl.pallas_call(
        _ssm_prep_kernel,
        out_shape=[
            jax.ShapeDtypeStruct((g, n), F32),
            jax.ShapeDtypeStruct((g, n), F32),
            jax.ShapeDtypeStruct((g, c, n), F32),
            jax.ShapeDtypeStruct((g, c, n), F32),
        ],
        name="ssm_prep",
    )(a_re, a_im, log_step, b_re_t, b_im_t)


def _meta_state_kernel(u_ref, wb0_ref, wb1_ref, lr_ref, li_ref, x0_ref, *, half):
    u = u_ref[...]
    hw = u.shape[1] // 2
    bu0 = _dot(u[:, :hw], wb0_ref[...])
    bu1 = _dot(u[:, hw:], wb1_ref[...])
    bre = jnp.concatenate([bu0[:, :half], bu1[:, :half]], axis=-1)
    bim = jnp.concatenate([bu0[:, half:], bu1[:, half:]], axis=-1)
    lr = lr_ref[...]
    li = li_ref[...]
    xr = jnp.zeros_like(lr)
    xi = jnp.zeros_like(li)
    for t in range(u.shape[0]):
        nr = lr * xr - li * xi + bre[t:t + 1, :]
        ni = lr * xi + li * xr + bim[t:t + 1, :]
        xr, xi = nr, ni
    x0_ref[...] = jnp.concatenate([xr, xi], axis=-1)


def _meta_state(u_meta, wb0, wb1, lam_re_row, lam_im_row):
    nstate = lam_re_row.shape[1]
    return pl.pallas_call(
        functools.partial(_meta_state_kernel, half=nstate // 2),
        out_shape=jax.ShapeDtypeStruct((1, 2 * nstate), F32),
        name="meta_state",
    )(u_meta, wb0, wb1, lam_re_row, lam_im_row)


def _gelu_tanh(y):
    c = math.sqrt(2.0 / math.pi)
    return 0.5 * y * (1.0 + jnp.tanh(c * (y + 0.044715 * (y * y * y))))


def _ssm_kernel(u_ref, gate_ref, x0_ref, wb0_ref, wb1_ref, lr_ref, li_ref,
                wcr0_ref, wci0_ref, wcr1_ref, wci1_ref, dsk_ref, ga_ref, gb_ref,
                o_ref, bu_ref, up_ref, ybm_ref, st_ref, *, tc, nstate, lane_chunk):
    nb = SSM_BATCH_ROWS
    half = nstate // 2
    width = u_ref.shape[2]
    hw = width // 2
    ngrp = tc // V7X_BF16_ROWS

    @pl.when(pl.program_id(1) == 0)
    def _():
        st_ref[...] = jnp.broadcast_to(x0_ref[...], st_ref.shape)

    ri = lax.broadcasted_iota(jnp.int32, (PERM_ROWS, PERM_ROWS), 0)
    ci = lax.broadcasted_iota(jnp.int32, (PERM_ROWS, PERM_ROWS), 1)
    src_of_row = (ri % nb) * V7X_BF16_ROWS + ri // nb
    perm = (ci == src_of_row).astype(BF16)
    dst_of_row = (ri % V7X_BF16_ROWS) * nb + ri // V7X_BF16_ROWS
    perm_t = (ci == dst_of_row).astype(BF16)

    for k in range(ngrp):
        xk = jnp.concatenate(
            [u_ref[b, pl.ds(k * V7X_BF16_ROWS, V7X_BF16_ROWS), :] for b in range(nb)], axis=0)
        up_ref[pl.ds(k * PERM_ROWS, PERM_ROWS), :] = _dot(perm, xk).astype(BF16)

    up = up_ref[...]
    bu0 = _dot(up[:, :hw], wb0_ref[...])
    bu_ref[:, 0:half] = bu0[:, :half]
    bu_ref[:, nstate:nstate + half] = bu0[:, half:]
    bu1 = _dot(up[:, hw:], wb1_ref[...])
    bu_ref[:, half:nstate] = bu1[:, :half]
    bu_ref[:, nstate + half:] = bu1[:, half:]

    for c in range(nstate // lane_chunk):
        re_l = pl.ds(c * lane_chunk, lane_chunk)
        im_l = pl.ds(nstate + c * lane_chunk, lane_chunk)
        lr = jnp.broadcast_to(lr_ref[:, re_l], (nb, lane_chunk))
        li = jnp.broadcast_to(li_ref[:, re_l], (nb, lane_chunk))

        def step(t, carry, re_l=re_l, im_l=im_l, lr=lr, li=li):
            xr, xi = carry
            r = pl.ds(pl.multiple_of(t * nb, nb), nb)
            nr = lr * xr - li * xi + bu_ref[r, re_l]
            ni = lr * xi + li * xr + bu_ref[r, im_l]
            bu_ref[r, re_l] = nr
            bu_ref[r, im_l] = ni
            return nr, ni

        xr, xi = lax.fori_loop(0, tc, step, (st_ref[:, re_l], st_ref[:, im_l]), unroll=4)
        st_ref[:, re_l] = xr
        st_ref[:, im_l] = xi

    xs_re0 = bu_ref[:, 0:half].astype(BF16)
    xs_im0 = bu_ref[:, nstate:nstate + half].astype(BF16)
    y0 = _dot(xs_re0, wcr0_ref[...]) + _dot(xs_im0, wci0_ref[...])
    xs_re1 = bu_ref[:, half:nstate].astype(BF16)
    xs_im1 = bu_ref[:, nstate + half:].astype(BF16)
    y1 = _dot(xs_re1, wcr1_ref[...]) + _dot(xs_im1, wci1_ref[...])
    y = jnp.concatenate([y0, y1], axis=-1) + dsk_ref[...] * up_ref[...].astype(F32)
    yg = _gelu_tanh(y).astype(BF16)

    for k in range(ngrp):
        yk = _dot(perm_t, yg[k * PERM_ROWS:(k + 1) * PERM_ROWS, :]).astype(BF16)
        for b in range(nb):
            ybm_ref[pl.ds(b * tc + k * V7X_BF16_ROWS, V7X_BF16_ROWS), :] = (
                yk[b * V7X_BF16_ROWS:(b + 1) * V7X_BF16_ROWS, :])

    ybm = ybm_ref[...]
    ssm = _dot(ybm, ga_ref[...]) * _sigmoid(_dot(ybm, gb_ref[...]))
    d = o_ref.shape[2]
    gate = gate_ref[...].astype(F32).reshape(nb * tc, d)
    o_ref[...] = (ssm * gate).reshape(nb, tc, d).astype(BF16)


def _ssm(u3, gate3, x0, wb0, wb1, lam_re_row, lam_im_row, wcr0, wci0, wcr1, wci1, dsk, ga, gb,
         *, tc, lane_chunk, vmem):
    batch, seq, width = u3.shape
    d = gate3.shape[2]
    nstate = lam_re_row.shape[1]
    nb = SSM_BATCH_ROWS
    tc = min(tc, seq)
    consts = [x0, wb0, wb1, lam_re_row, lam_im_row, wcr0, wci0, wcr1, wci1, dsk, ga, gb]
    return pl.pallas_call(
        functools.partial(_ssm_kernel, tc=tc, nstate=nstate, lane_chunk=lane_chunk),
        grid=(batch // nb, seq // tc),
        in_specs=[
            pl.BlockSpec((nb, tc, width), lambda g, t: (g, t, 0)),
            pl.BlockSpec((nb, tc, d), lambda g, t: (g, t, 0)),
        ] + [_const_spec(c.shape) for c in consts],
        out_specs=pl.BlockSpec((nb, tc, d), lambda g, t: (g, t, 0)),
        out_shape=jax.ShapeDtypeStruct((batch, seq, d), BF16),
        scratch_shapes=[
            pltpu.VMEM((nb * tc, 2 * nstate), F32),
            pltpu.VMEM((nb * tc, width), BF16),
            pltpu.VMEM((nb * tc, width), BF16),
            pltpu.VMEM((nb, 2 * nstate), F32),
        ],
        compiler_params=pltpu.CompilerParams(
            dimension_semantics=("parallel", "arbitrary"), vmem_limit_bytes=vmem),
        name="ssm",
    )(u3, gate3, *consts)


def _block_diag(blocks):
    g, r, c = blocks.shape
    eye = jnp.eye(g, dtype=blocks.dtype)
    return jnp.einsum("grc,gh->grhc", blocks, eye).reshape(g * r, g * c)


def kernel(x, meta_tokens, ffn1_norm, ffn1_w1, ffn1_w3, ffn1_w2, mix_norm, w_in, attn_sinks, ssm_a_re, ssm_a_im, ssm_log_step, ssm_b_re, ssm_b_im, ssm_c_re, ssm_c_im, ssm_d, ssm_glu_a, ssm_glu_b, w_out, ffn2_norm, ffn2_w1, ffn2_w3, ffn2_w2, final_norm):
    batch, seq, d = x.shape
    n_q = d // HEAD_DIM
    n_kv = n_q // Q_PER_KV
    kvw = n_kv * HEAD_DIM
    width = d // 2
    groups = width // SSM_GROUP
    nstate = groups * SSM_STATE
    half = nstate // 2
    hw = width // 2
    assert seq % WINDOW == 0 and batch % SSM_BATCH_ROWS == 0
    vmem = V7X_VMEM_BYTES - 8 * 1024 * 1024

    bf = lambda w: w.astype(BF16)
    row = lambda v: v.reshape(1, -1).astype(F32)

    lam_re, lam_im, bb_re_t, bb_im_t = _ssm_prep(
        ssm_a_re[0].astype(F32), ssm_a_im[0].astype(F32), ssm_log_step[0].astype(F32).reshape(groups, 1),
        jnp.swapaxes(ssm_b_re[0].astype(F32), 1, 2), jnp.swapaxes(ssm_b_im[0].astype(F32), 1, 2))
    bd_re = _block_diag(bb_re_t)
    bd_im = _block_diag(bb_im_t)
    wb0 = bf(jnp.concatenate([bd_re[:hw, :half], bd_im[:hw, :half]], axis=1))
    wb1 = bf(jnp.concatenate([bd_re[hw:, half:], bd_im[hw:, half:]], axis=1))
    cd_re = _block_diag(jnp.swapaxes(ssm_c_re[0].astype(F32), 1, 2))
    cd_im = _block_diag(jnp.swapaxes(-ssm_c_im[0].astype(F32), 1, 2))
    wcr0, wci0 = bf(cd_re[:half, :hw]), bf(cd_im[:half, :hw])
    wcr1, wci1 = bf(cd_re[half:, hw:]), bf(cd_im[half:, hw:])
    lam_re_row = lam_re.reshape(1, nstate)
    lam_im_row = lam_im.reshape(1, nstate)

    w1a, w3a, w2a = bf(ffn1_w1[0]), bf(ffn1_w3[0]), bf(ffn1_w2[0])
    g1, gm = row(ffn1_norm[0]), row(mix_norm[0])
    win = w_in[0]
    c_k, c_v, c_u = d, d + kvw, d + 2 * kvw
    wcol = bf(jnp.concatenate([win[:, :c_k] * (HEAD_DIM ** -0.5 * LOG2E), win[:, c_v:c_u]], axis=1).T)
    wrow = bf(jnp.concatenate([win[:, c_k:c_v], win[:, c_u:]], axis=1))
    ffn_in = functools.partial(_ffn_in, g1=g1, w1=w1a, w3=w3a, w2=w2a, gm=gm, wrow=wrow, wcol=wcol,
                               d_kv=kvw, d_u=width, vmem=vmem)

    _, _, vt_meta, k_meta, u_meta, _, _ = ffn_in(meta_tokens.astype(F32), tm=N_META)
    x0 = _meta_state(u_meta, wb0, wb1, lam_re_row, lam_im_row)

    rows = batch * seq
    h1, qt, vt, k, u, gate_a, gate_s = ffn_in(x.reshape(rows, d), tm=256)

    attn_g = _attention(attn_sinks[0].astype(F32), qt, k, vt, k_meta, vt_meta, gate_a,
                        batch=batch, seq=seq, n_kv=n_kv, tq=512, vmem=vmem)
    ssm_g = _ssm(u.reshape(batch, seq, width), gate_s.reshape(batch, seq, d), x0, wb0, wb1,
                 lam_re_row, lam_im_row, wcr0, wci0, wcr1, wci1, row(ssm_d[0]),
                 bf(ssm_glu_a[0]), bf(ssm_glu_b[0]), tc=128, lane_chunk=512, vmem=vmem)

    out = _ffn_out(h1, attn_g, ssm_g.reshape(rows, d), bf(w_out[0]), row(ffn2_norm[0]),
                   bf(ffn2_w1[0]), bf(ffn2_w3[0]), bf(ffn2_w2[0]), row(final_norm), tm=512, vmem=vmem)
    return out.reshape(batch, seq, d)
```

```python
import functools
import math

import jax
import jax.numpy as jnp
from jax import lax
from jax.experimental import pallas as pl
from jax.experimental.pallas import tpu as pltpu

F32 = jnp.float32
BF16 = jnp.bfloat16

N_META = 16
HEAD_DIM = 64
Q_PER_KV = 4
WINDOW = 128
SSM_GROUP = 16
SSM_STATE = 64
NORM_EPS = 1e-6
NEG_INF = -1e30
LOG2E = math.log2(math.e)

V7X_VMEM_BYTES = 64 * 1024 * 1024
V7X_SUBLANES = 8
V7X_BF16_ROWS = 16
SSM_BATCH_ROWS = V7X_SUBLANES
PERM_ROWS = V7X_BF16_ROWS * SSM_BATCH_ROWS
SSM_OUT_SLICES = 2


def _const_spec(shape):
    nd = len(shape)
    return pl.BlockSpec(shape, lambda *_: (0,) * nd, pipeline_mode=pl.Buffered(1))


def _dot(a, b):
    return jnp.dot(a, b, preferred_element_type=F32)


def _dot_nt(a, b):
    return lax.dot_general(a, b, (((1,), (1,)), ((), ())), preferred_element_type=F32)


def _rms(x, g):
    return x * lax.rsqrt(jnp.mean(x * x, axis=-1, keepdims=True) + NORM_EPS) * g


def _sigmoid(x):
    return 0.5 * jnp.tanh(0.5 * x) + 0.5


def _swiglu(xn_bf16, w1_ref, w3_ref, w2_ref):
    a = _dot(xn_bf16, w1_ref[...])
    b = _dot(xn_bf16, w3_ref[...])
    act = (a * _sigmoid(a) * b).astype(BF16)
    return _dot(act, w2_ref[...])


def _ffn1_kernel(x_ref, g1_ref, w1_ref, w3_ref, w2_ref, h_ref):
    x = x_ref[...]
    h_ref[...] = x + 0.5 * _swiglu(_rms(x, g1_ref[...]).astype(BF16), w1_ref, w3_ref, w2_ref)


def _ffn1(x2d, g1, w1, w3, w2, *, tm, vmem):
    rows, d = x2d.shape
    dff = w1.shape[1]
    tm = min(tm, rows)
    tok = pl.BlockSpec((tm, d), lambda i: (i, 0))
    return pl.pallas_call(
        _ffn1_kernel,
        grid=(rows // tm,),
        in_specs=[tok, _const_spec((1, d)), _const_spec((d, dff)), _const_spec((d, dff)),
                  _const_spec((dff, d))],
        out_specs=tok,
        out_shape=jax.ShapeDtypeStruct((rows, d), F32),
        compiler_params=pltpu.CompilerParams(
            dimension_semantics=("parallel",), vmem_limit_bytes=vmem),
        name="ffn1",
    )(x2d, g1, w1, w3, w2)


def _proj_kernel(h_ref, gm_ref, wrow_ref, wcol_ref, qt_ref, vt_ref, k_ref, u_ref, ga_ref, gs_ref):
    hn = _rms(h_ref[...], gm_ref[...]).astype(BF16)
    pr = _dot(hn, wrow_ref[...])
    c0 = 0
    for ref, is_gate in ((k_ref, False), (u_ref, False), (ga_ref, True), (gs_ref, True)):
        c1 = c0 + ref.shape[1]
        piece = pr[:, c0:c1]
        ref[...] = (_sigmoid(piece) if is_gate else piece).astype(BF16)
        c0 = c1
    pt = _dot_nt(wcol_ref[...], hn)
    nq = qt_ref.shape[0]
    qt_ref[...] = pt[:nq, :].astype(BF16)
    vt_ref[...] = pt[nq:, :].astype(BF16)


def _proj(h2d, gm, wrow, wcol, *, d_kv, d_u, tm, vmem):
    rows, d = h2d.shape
    tm = min(tm, rows)
    tok = lambda w: pl.BlockSpec((tm, w), lambda i: (i, 0))
    feat = lambda w: pl.BlockSpec((w, tm), lambda i: (0, i))
    return pl.pallas_call(
        _proj_kernel,
        grid=(rows // tm,),
        in_specs=[
            tok(d),
            _const_spec((1, d)),
            _const_spec(wrow.shape),
            _const_spec(wcol.shape),
        ],
        out_specs=[feat(d), feat(d_kv), tok(d_kv), tok(d_u), tok(d), tok(d)],
        out_shape=[
            jax.ShapeDtypeStruct((d, rows), BF16),
            jax.ShapeDtypeStruct((d_kv, rows), BF16),
            jax.ShapeDtypeStruct((rows, d_kv), BF16),
            jax.ShapeDtypeStruct((rows, d_u), BF16),
            jax.ShapeDtypeStruct((rows, d), BF16),
            jax.ShapeDtypeStruct((rows, d), BF16),
        ],
        compiler_params=pltpu.CompilerParams(
            dimension_semantics=("parallel",), vmem_limit_bytes=vmem),
        name="proj",
    )(h2d, gm, wrow, wcol)


def _ffn_out_kernel(h_ref, ag_ref, sg_ref, wout_ref, g2_ref, w1_ref, w3_ref, w2_ref, gf_ref, o_ref):
    merged = (ag_ref[...].astype(F32) + sg_ref[...].astype(F32)).astype(BF16)
    h = h_ref[...] + _dot(merged, wout_ref[...])
    h = h + 0.5 * _swiglu(_rms(h, g2_ref[...]).astype(BF16), w1_ref, w3_ref, w2_ref)
    o_ref[...] = _rms(h, gf_ref[...])


def _ffn_out(h1, ag, sg, wout, g2, w1, w3, w2, gf, *, tm, vmem):
    rows, d = h1.shape
    dff = w1.shape[1]
    tm = min(tm, rows)
    row_spec = pl.BlockSpec((tm, d), lambda i: (i, 0))
    return pl.pallas_call(
        _ffn_out_kernel,
        grid=(rows // tm,),
        in_specs=[
            row_spec, row_spec, row_spec,
            _const_spec((d, d)),
            _const_spec((1, d)),
            _const_spec((d, dff)),
            _const_spec((d, dff)),
            _const_spec((dff, d)),
            _const_spec((1, d)),
        ],
        out_specs=row_spec,
        out_shape=jax.ShapeDtypeStruct((rows, d), F32),
        compiler_params=pltpu.CompilerParams(
            dimension_semantics=("parallel",), vmem_limit_bytes=vmem),
        name="ffn_out",
    )(h1, ag, sg, wout, g2, w1, w3, w2, gf)


def _attn_kernel(sink_ref, qt_ref, kc_ref, kp_ref, km_ref, vtc_ref, vtp_ref, vtm_ref, gate_ref, o_ref,
                 *, tq, n_kv):
    nsub = tq // WINDOW
    first_block = pl.program_id(1) * nsub
    lanes4 = Q_PER_KV * WINDOW
    kj = lax.broadcasted_iota(jnp.int32, (WINDOW, lanes4), 0)
    qi = lax.broadcasted_iota(jnp.int32, (WINDOW, lanes4), 1) % WINDOW
    cur_ok = kj <= qi
    cur_f = cur_ok.astype(F32)
    pair_w = 2 * HEAD_DIM

    def cols_of(j):
        return pl.ds(j * WINDOW, WINDOW)

    def scores(j, kh):
        if j == 0:
            k_prev = kp_ref[...]
        else:
            k_prev = kc_ref[cols_of(j - 1), :]
        k_all = jnp.concatenate([kc_ref[cols_of(j), :], k_prev, km_ref[...]], axis=0)
        qt4 = jnp.concatenate(
            [qt_ref[pl.ds((kh * Q_PER_KV + g) * HEAD_DIM, HEAD_DIM), cols_of(j)] for g in range(Q_PER_KV)],
            axis=1)
        zeros = jnp.zeros_like(qt4)
        w = jnp.concatenate([qt4, zeros] if kh % 2 == 0 else [zeros, qt4], axis=0)
        pair = kh // 2
        return _dot(k_all[:, pair * pair_w:(pair + 1) * pair_w], w)

    tasks = [(j, kh) for j in range(nsub) for kh in range(n_kv)]
    s_next = scores(*tasks[0])
    head_out = []
    for idx, (j, kh) in enumerate(tasks):
        s_all = s_next
        if idx + 1 < len(tasks):
            s_next = scores(*tasks[idx + 1])
        s_prev = s_all[WINDOW:2 * WINDOW]
        if j == 0:
            s_prev = s_prev + jnp.where(first_block > 0, 0.0, NEG_INF).astype(F32)
        s = jnp.where(cur_ok, s_all[:WINDOW], s_prev)
        s_meta = s_all[2 * WINDOW:]
        sink = jnp.concatenate(
            [jnp.full((1, WINDOW), sink_ref[kh * Q_PER_KV + g] * LOG2E, F32) for g in range(Q_PER_KV)],
            axis=1)
        m = jnp.maximum(jnp.maximum(jnp.max(s, axis=0, keepdims=True),
                                    jnp.max(s_meta, axis=0, keepdims=True)), sink)
        p = jnp.exp2(s - m)
        pm = jnp.exp2(s_meta - m)
        denom = (jnp.sum(p, axis=0, keepdims=True) + jnp.sum(pm, axis=0, keepdims=True)
                 + jnp.exp2(sink - m))
        p_cur = p * cur_f
        p_prev = p - p_cur
        hs = slice(kh * HEAD_DIM, (kh + 1) * HEAD_DIM)
        if j == 0:
            vt_prev = vtp_ref[hs, :]
        else:
            vt_prev = vtc_ref[hs, cols_of(j - 1)]
        ot = (_dot(vtc_ref[hs, cols_of(j)], p_cur.astype(BF16))
              + _dot(vt_prev, p_prev.astype(BF16))
              + _dot(vtm_ref[hs, :], pm.astype(BF16)))
        ot = ot * (1.0 / denom)
        for g in range(Q_PER_KV):
            head_out.append(ot[:, g * WINDOW:(g + 1) * WINDOW])
        if kh == n_kv - 1:
            attn = jnp.concatenate(head_out, axis=0).T
            o_ref[cols_of(j), :] = (attn * gate_ref[cols_of(j), :].astype(F32)).astype(BF16)
            head_out = []


def _attention(sinks, qt, k, vt, k_meta, vt_meta, gate, *, batch, seq, n_kv, tq, vmem):
    d, rows = qt.shape
    kvw = n_kv * HEAD_DIM
    tq = min(tq, seq)
    nsub = tq // WINDOW
    steps = seq // tq
    blocks = seq // WINDOW

    def prev_block(b, i):
        return jnp.maximum(b * blocks + i * nsub - 1, 0)

    return pl.pallas_call(
        functools.partial(_attn_kernel, tq=tq, n_kv=n_kv),
        grid=(batch, steps),
        in_specs=[
            pl.BlockSpec(memory_space=pltpu.SMEM),
            pl.BlockSpec((d, tq), lambda b, i: (0, b * steps + i)),
            pl.BlockSpec((tq, kvw), lambda b, i: (b * steps + i, 0)),
            pl.BlockSpec((WINDOW, kvw), lambda b, i: (prev_block(b, i), 0)),
            _const_spec((N_META, kvw)),
            pl.BlockSpec((kvw, tq), lambda b, i: (0, b * steps + i)),
            pl.BlockSpec((kvw, WINDOW), lambda b, i: (0, prev_block(b, i))),
            _const_spec((kvw, N_META)),
            pl.BlockSpec((tq, d), lambda b, i: (b * steps + i, 0)),
        ],
        out_specs=pl.BlockSpec((tq, d), lambda b, i: (b * steps + i, 0)),
        out_shape=jax.ShapeDtypeStruct((rows, d), BF16),
        compiler_params=pltpu.CompilerParams(
            dimension_semantics=("parallel", "parallel"), vmem_limit_bytes=vmem),
        name="attention",
    )(sinks, qt, k, k, k_meta, vt, vt, vt_meta, gate)


def _ssm_prep_kernel(ar_ref, ai_ref, ls_ref, br_ref, bi_ref, lr_ref, li_ref, bbr_ref, bbi_ref):
    ar = ar_ref[...]
    ai = ai_ref[...]
    step = jnp.exp(ls_ref[...])
    mag = jnp.exp(ar * step)
    ang = ai * step
    lam_re = mag * jnp.cos(ang)
    lam_im = mag * jnp.sin(ang)
    den = ar * ar + ai * ai
    nr = lam_re - 1.0
    ni = lam_im
    coef_re = (nr * ar + ni * ai) / den
    coef_im = (ni * ar - nr * ai) / den
    lr_ref[...] = lam_re
    li_ref[...] = lam_im
    br = br_ref[...]
    bi = bi_ref[...]
    cr = coef_re[:, None, :]
    ci = coef_im[:, None, :]
    bbr_ref[...] = cr * br - ci * bi
    bbi_ref[...] = cr * bi + ci * br


def _ssm_prep(a_re, a_im, log_step, b_re_t, b_im_t):
    g, n = a_re.shape
    c = b_re_t.shape[1]
    return pl.pallas_call(
        _ssm_prep_kernel,
        out_shape=[
            jax.ShapeDtypeStruct((g, n), F32),
            jax.ShapeDtypeStruct((g, n), F32),
            jax.ShapeDtypeStruct((g, c, n), F32),
            jax.ShapeDtypeStruct((g, c, n), F32),
        ],
        name="ssm_prep",
    )(a_re, a_im, log_step, b_re_t, b_im_t)


def _meta_state_kernel(u_ref, wb0_ref, wb1_ref, lr_ref, li_ref, x0_ref, *, half):
    u = u_ref[...]
    hw = u.shape[1] // 2
    bu0 = _dot(u[:, :hw], wb0_ref[...])
    bu1 = _dot(u[:, hw:], wb1_ref[...])
    bre = jnp.concatenate([bu0[:, :half], bu1[:, :half]], axis=-1)
    bim = jnp.concatenate([bu0[:, half:], bu1[:, half:]], axis=-1)
    lr = lr_ref[...]
    li = li_ref[...]
    xr = jnp.zeros_like(lr)
    xi = jnp.zeros_like(li)
    for t in range(u.shape[0]):
        nr = lr * xr - li * xi + bre[t:t + 1, :]
        ni = lr * xi + li * xr + bim[t:t + 1, :]
        xr, xi = nr, ni
    x0_ref[...] = jnp.concatenate([xr, xi], axis=-1)


def _meta_state(u_meta, wb0, wb1, lam_re_row, lam_im_row):
    nstate = lam_re_row.shape[1]
    return pl.pallas_call(
        functools.partial(_meta_state_kernel, half=nstate // 2),
        out_shape=jax.ShapeDtypeStruct((1, 2 * nstate), F32),
        name="meta_state",
    )(u_meta, wb0, wb1, lam_re_row, lam_im_row)


def _gelu_tanh(y):
    c = math.sqrt(2.0 / math.pi)
    return 0.5 * y * (1.0 + jnp.tanh(c * (y + 0.044715 * (y * y * y))))


def _ssm_kernel(u_ref, gate_ref, x0_ref, wb0_ref, wb1_ref, lr_ref, li_ref,
                wcr0_ref, wci0_ref, wcr1_ref, wci1_ref, dsk_ref, ga_ref, gb_ref,
                o_ref, bu_ref, xs_ref, up_ref, ybm_ref, st_ref, *, tc, nstate, lane_chunk):
    nb = SSM_BATCH_ROWS
    half = nstate // 2
    width = u_ref.shape[2]
    hw = width // 2
    ngrp = tc // V7X_BF16_ROWS

    @pl.when(pl.program_id(1) == 0)
    def _():
        st_ref[...] = jnp.broadcast_to(x0_ref[...], st_ref.shape)

    ri = lax.broadcasted_iota(jnp.int32, (PERM_ROWS, PERM_ROWS), 0)
    ci = lax.broadcasted_iota(jnp.int32, (PERM_ROWS, PERM_ROWS), 1)
    src_of_row = (ri % nb) * V7X_BF16_ROWS + ri // nb
    perm = (ci == src_of_row).astype(BF16)
    dst_of_row = (ri % V7X_BF16_ROWS) * nb + ri // V7X_BF16_ROWS
    perm_t = (ci == dst_of_row).astype(BF16)

    for k in range(ngrp):
        xk = jnp.concatenate(
            [u_ref[b, pl.ds(k * V7X_BF16_ROWS, V7X_BF16_ROWS), :] for b in range(nb)], axis=0)
        up_ref[pl.ds(k * PERM_ROWS, PERM_ROWS), :] = _dot(perm, xk).astype(BF16)

    up = up_ref[...]
    bu0 = _dot(up[:, :hw], wb0_ref[...])
    bu_ref[:, 0:half] = bu0[:, :half]
    bu_ref[:, nstate:nstate + half] = bu0[:, half:]
    bu1 = _dot(up[:, hw:], wb1_ref[...])
    bu_ref[:, half:nstate] = bu1[:, :half]
    bu_ref[:, nstate + half:] = bu1[:, half:]

    for c in range(nstate // lane_chunk):
        re_l = pl.ds(c * lane_chunk, lane_chunk)
        im_l = pl.ds(nstate + c * lane_chunk, lane_chunk)
        lr = jnp.broadcast_to(lr_ref[:, re_l], (nb, lane_chunk))
        li = jnp.broadcast_to(li_ref[:, re_l], (nb, lane_chunk))

        def step(t, carry, re_l=re_l, im_l=im_l, lr=lr, li=li):
            xr, xi = carry
            r = pl.ds(pl.multiple_of(t * nb, nb), nb)
            nr = lr * xr - li * xi + bu_ref[r, re_l]
            ni = lr * xi + li * xr + bu_ref[r, im_l]
            xs_ref[r, re_l] = nr
            xs_ref[r, im_l] = ni
            return nr, ni

        xr, xi = lax.fori_loop(0, tc, step, (st_ref[:, re_l], st_ref[:, im_l]), unroll=4)
        st_ref[:, re_l] = xr
        st_ref[:, im_l] = xi

    nslice = ybm_ref.shape[0]
    ts = tc // nslice
    rs = nb * ts
    d = o_ref.shape[2]

    def readout(s):
        rows = pl.ds(s * rs, rs)
        y0 = (_dot(xs_ref[rows, 0:half].astype(BF16), wcr0_ref[...])
              + _dot(xs_ref[rows, nstate:nstate + half].astype(BF16), wci0_ref[...]))
        y1 = (_dot(xs_ref[rows, half:nstate].astype(BF16), wcr1_ref[...])
              + _dot(xs_ref[rows, nstate + half:].astype(BF16), wci1_ref[...]))
        return jnp.concatenate([y0, y1], axis=-1) + dsk_ref[...] * up_ref[rows, :].astype(F32)

    def activate(s, y):
        yg = _gelu_tanh(y).astype(BF16)
        for k in range(ts // V7X_BF16_ROWS):
            yk = _dot(perm_t, yg[k * PERM_ROWS:(k + 1) * PERM_ROWS, :]).astype(BF16)
            for b in range(nb):
                ybm_ref[s, pl.ds(b * ts + k * V7X_BF16_ROWS, V7X_BF16_ROWS), :] = (
                    yk[b * V7X_BF16_ROWS:(b + 1) * V7X_BF16_ROWS, :])

    def emit(s, a, g):
        t_sl = pl.ds(s * ts, ts)
        gate = gate_ref[:, t_sl, :].astype(F32).reshape(rs, d)
        o_ref[:, t_sl, :] = (a * _sigmoid(g) * gate).reshape(nb, ts, d).astype(BF16)

    ys = [readout(s) for s in range(nslice)]
    activate(0, ys[0])
    for s in range(nslice):
        ybm = ybm_ref[s]
        a = _dot(ybm, ga_ref[...])
        g = _dot(ybm, gb_ref[...])
        if s + 1 < nslice:
            activate(s + 1, ys[s + 1])
        emit(s, a, g)


def _ssm(u3, gate3, x0, wb0, wb1, lam_re_row, lam_im_row, wcr0, wci0, wcr1, wci1, dsk, ga, gb,
         *, tc, lane_chunk, vmem):
    batch, seq, width = u3.shape
    d = gate3.shape[2]
    nstate = lam_re_row.shape[1]
    nb = SSM_BATCH_ROWS
    tc = min(tc, seq)
    consts = [x0, wb0, wb1, lam_re_row, lam_im_row, wcr0, wci0, wcr1, wci1, dsk, ga, gb]
    return pl.pallas_call(
        functools.partial(_ssm_kernel, tc=tc, nstate=nstate, lane_chunk=lane_chunk),
        grid=(batch // nb, seq // tc),
        in_specs=[
            pl.BlockSpec((nb, tc, width), lambda g, t: (g, t, 0)),
            pl.BlockSpec((nb, tc, d), lambda g, t: (g, t, 0)),
        ] + [_const_spec(c.shape) for c in consts],
        out_specs=pl.BlockSpec((nb, tc, d), lambda g, t: (g, t, 0)),
        out_shape=jax.ShapeDtypeStruct((batch, seq, d), BF16),
        scratch_shapes=[
            pltpu.VMEM((nb * tc, 2 * nstate), F32),
            pltpu.VMEM((nb * tc, 2 * nstate), F32),
            pltpu.VMEM((nb * tc, width), BF16),
            pltpu.VMEM((SSM_OUT_SLICES, nb * tc // SSM_OUT_SLICES, width), BF16),
            pltpu.VMEM((nb, 2 * nstate), F32),
        ],
        compiler_params=pltpu.CompilerParams(
            dimension_semantics=("parallel", "arbitrary"), vmem_limit_bytes=vmem),
        name="ssm",
    )(u3, gate3, *consts)


def _block_diag(blocks):
    g, r, c = blocks.shape
    eye = jnp.eye(g, dtype=blocks.dtype)
    return jnp.einsum("grc,gh->grhc", blocks, eye).reshape(g * r, g * c)


def kernel(x, meta_tokens, ffn1_norm, ffn1_w1, ffn1_w3, ffn1_w2, mix_norm, w_in, attn_sinks, ssm_a_re, ssm_a_im, ssm_log_step, ssm_b_re, ssm_b_im, ssm_c_re, ssm_c_im, ssm_d, ssm_glu_a, ssm_glu_b, w_out, ffn2_norm, ffn2_w1, ffn2_w3, ffn2_w2, final_norm):
    batch, seq, d = x.shape
    n_q = d // HEAD_DIM
    n_kv = n_q // Q_PER_KV
    kvw = n_kv * HEAD_DIM
    width = d // 2
    groups = width // SSM_GROUP
    nstate = groups * SSM_STATE
    half = nstate // 2
    hw = width // 2
    assert seq % WINDOW == 0 and batch % SSM_BATCH_ROWS == 0
    vmem = V7X_VMEM_BYTES - 8 * 1024 * 1024

    bf = lambda w: w.astype(BF16)
    row = lambda v: v.reshape(1, -1).astype(F32)

    lam_re, lam_im, bb_re_t, bb_im_t = _ssm_prep(
        ssm_a_re[0].astype(F32), ssm_a_im[0].astype(F32), ssm_log_step[0].astype(F32).reshape(groups, 1),
        jnp.swapaxes(ssm_b_re[0].astype(F32), 1, 2), jnp.swapaxes(ssm_b_im[0].astype(F32), 1, 2))
    bd_re = _block_diag(bb_re_t)
    bd_im = _block_diag(bb_im_t)
    wb0 = bf(jnp.concatenate([bd_re[:hw, :half], bd_im[:hw, :half]], axis=1))
    wb1 = bf(jnp.concatenate([bd_re[hw:, half:], bd_im[hw:, half:]], axis=1))
    cd_re = _block_diag(jnp.swapaxes(ssm_c_re[0].astype(F32), 1, 2))
    cd_im = _block_diag(jnp.swapaxes(-ssm_c_im[0].astype(F32), 1, 2))
    wcr0, wci0 = bf(cd_re[:half, :hw]), bf(cd_im[:half, :hw])
    wcr1, wci1 = bf(cd_re[half:, hw:]), bf(cd_im[half:, hw:])
    lam_re_row = lam_re.reshape(1, nstate)
    lam_im_row = lam_im.reshape(1, nstate)

    w1a, w3a, w2a = bf(ffn1_w1[0]), bf(ffn1_w3[0]), bf(ffn1_w2[0])
    g1, gm = row(ffn1_norm[0]), row(mix_norm[0])
    win = w_in[0]
    c_k, c_v, c_u = d, d + kvw, d + 2 * kvw
    wcol = bf(jnp.concatenate([win[:, :c_k] * (HEAD_DIM ** -0.5 * LOG2E), win[:, c_v:c_u]], axis=1).T)
    wrow = bf(jnp.concatenate([win[:, c_k:c_v], win[:, c_u:]], axis=1))
    ffn1 = functools.partial(_ffn1, g1=g1, w1=w1a, w3=w3a, w2=w2a, vmem=vmem)
    proj = functools.partial(_proj, gm=gm, wrow=wrow, wcol=wcol, d_kv=kvw, d_u=width, vmem=vmem)

    _, vt_meta, k_meta, u_meta, _, _ = proj(ffn1(meta_tokens.astype(F32), tm=N_META), tm=N_META)
    x0 = _meta_state(u_meta, wb0, wb1, lam_re_row, lam_im_row)

    rows = batch * seq
    h1 = ffn1(x.reshape(rows, d), tm=512)
    qt, vt, k, u, gate_a, gate_s = proj(h1, tm=512)

    attn_g = _attention(attn_sinks[0].astype(F32), qt, k, vt, k_meta, vt_meta, gate_a,
                        batch=batch, seq=seq, n_kv=n_kv, tq=512, vmem=vmem)
    ssm_g = _ssm(u.reshape(batch, seq, width), gate_s.reshape(batch, seq, d), x0, wb0, wb1,
                 lam_re_row, lam_im_row, wcr0, wci0, wcr1, wci1, row(ssm_d[0]),
                 bf(ssm_glu_a[0]), bf(ssm_glu_b[0]), tc=128, lane_chunk=512, vmem=vmem)

    out = _ffn_out(h1, attn_g, ssm_g.reshape(rows, d), bf(w_out[0]), row(ffn2_norm[0]),
                   bf(ffn2_w1[0]), bf(ffn2_w3[0]), bf(ffn2_w2[0]), row(final_norm), tm=512, vmem=vmem)
    return out.reshape(batch, seq, d)
```

```python
import functools
import math

import jax
import jax.numpy as jnp
from jax import lax
from jax.experimental import pallas as pl
from jax.experimental.pallas import tpu as pltpu

F32 = jnp.float32
BF16 = jnp.bfloat16

N_META = 16
HEAD_DIM = 64
Q_PER_KV = 4
WINDOW = 128
SSM_GROUP = 16
SSM_STATE = 64
NORM_EPS = 1e-6
NEG_INF = -1e30
LOG2E = math.log2(math.e)

V7X_VMEM_BYTES = 64 * 1024 * 1024
V7X_SUBLANES = 8
V7X_BF16_ROWS = 16
SSM_BATCH_ROWS = V7X_SUBLANES
PERM_ROWS = V7X_BF16_ROWS * SSM_BATCH_ROWS
SSM_OUT_SLICES = 2


def _const_spec(shape):
    nd = len(shape)
    return pl.BlockSpec(shape, lambda *_: (0,) * nd, pipeline_mode=pl.Buffered(1))


def _dot(a, b):
    return jnp.dot(a, b, preferred_element_type=F32)


def _dot_nt(a, b):
    return lax.dot_general(a, b, (((1,), (1,)), ((), ())), preferred_element_type=F32)


def _rms(x, g):
    return x * lax.rsqrt(jnp.mean(x * x, axis=-1, keepdims=True) + NORM_EPS) * g


def _sigmoid(x):
    return 0.5 * jnp.tanh(0.5 * x) + 0.5


def _swiglu(xn_bf16, w1_ref, w3_ref, w2_ref):
    a = _dot(xn_bf16, w1_ref[...])
    b = _dot(xn_bf16, w3_ref[...])
    act = (a * _sigmoid(a) * b).astype(BF16)
    return _dot(act, w2_ref[...])


def _ffn1_kernel(x_ref, g1_ref, w1_ref, w3_ref, w2_ref, h_ref):
    x = x_ref[...]
    h_ref[...] = x + 0.5 * _swiglu(_rms(x, g1_ref[...]).astype(BF16), w1_ref, w3_ref, w2_ref)


def _ffn1(x2d, g1, w1, w3, w2, *, tm, vmem):
    rows, d = x2d.shape
    dff = w1.shape[1]
    tm = min(tm, rows)
    tok = pl.BlockSpec((tm, d), lambda i: (i, 0))
    return pl.pallas_call(
        _ffn1_kernel,
        grid=(rows // tm,),
        in_specs=[tok, _const_spec((1, d)), _const_spec((d, dff)), _const_spec((d, dff)),
                  _const_spec((dff, d))],
        out_specs=tok,
        out_shape=jax.ShapeDtypeStruct((rows, d), F32),
        compiler_params=pltpu.CompilerParams(
            dimension_semantics=("parallel",), vmem_limit_bytes=vmem),
        name="ffn1",
    )(x2d, g1, w1, w3, w2)


def _proj_kernel(h_ref, gm_ref, wrow_ref, wcol_ref, qt_ref, vt_ref, k_ref, u_ref, ga_ref, gs_ref):
    hn = _rms(h_ref[...], gm_ref[...]).astype(BF16)
    pr = _dot(hn, wrow_ref[...])
    c0 = 0
    for ref, is_gate in ((k_ref, False), (u_ref, False), (ga_ref, True), (gs_ref, True)):
        c1 = c0 + ref.shape[1]
        piece = pr[:, c0:c1]
        ref[...] = (_sigmoid(piece) if is_gate else piece).astype(BF16)
        c0 = c1
    pt = _dot_nt(wcol_ref[...], hn)
    nq = qt_ref.shape[0]
    qt_ref[...] = pt[:nq, :].astype(BF16)
    vt_ref[...] = pt[nq:, :].astype(BF16)


def _proj(h2d, gm, wrow, wcol, *, d_kv, d_u, tm, vmem):
    rows, d = h2d.shape
    tm = min(tm, rows)
    tok = lambda w: pl.BlockSpec((tm, w), lambda i: (i, 0))
    feat = lambda w: pl.BlockSpec((w, tm), lambda i: (0, i))
    return pl.pallas_call(
        _proj_kernel,
        grid=(rows // tm,),
        in_specs=[
            tok(d),
            _const_spec((1, d)),
            _const_spec(wrow.shape),
            _const_spec(wcol.shape),
        ],
        out_specs=[feat(d), feat(d_kv), tok(d_kv), tok(d_u), tok(d), tok(d)],
        out_shape=[
            jax.ShapeDtypeStruct((d, rows), BF16),
            jax.ShapeDtypeStruct((d_kv, rows), BF16),
            jax.ShapeDtypeStruct((rows, d_kv), BF16),
            jax.ShapeDtypeStruct((rows, d_u), BF16),
            jax.ShapeDtypeStruct((rows, d), BF16),
            jax.ShapeDtypeStruct((rows, d), BF16),
        ],
        compiler_params=pltpu.CompilerParams(
            dimension_semantics=("parallel",), vmem_limit_bytes=vmem),
        name="proj",
    )(h2d, gm, wrow, wcol)


def _ffn_out_kernel(h_ref, ag_ref, sg_ref, wout_ref, g2_ref, w1_ref, w3_ref, w2_ref, gf_ref, o_ref):
    merged = (ag_ref[...].astype(F32) + sg_ref[...].astype(F32)).astype(BF16)
    h = h_ref[...] + _dot(merged, wout_ref[...])
    h = h + 0.5 * _swiglu(_rms(h, g2_ref[...]).astype(BF16), w1_ref, w3_ref, w2_ref)
    o_ref[...] = _rms(h, gf_ref[...])


def _ffn_out(h1, ag, sg, wout, g2, w1, w3, w2, gf, *, tm, vmem):
    rows, d = h1.shape
    dff = w1.shape[1]
    tm = min(tm, rows)
    row_spec = pl.BlockSpec((tm, d), lambda i: (i, 0))
    return pl.pallas_call(
        _ffn_out_kernel,
        grid=(rows // tm,),
        in_specs=[
            row_spec, row_spec, row_spec,
            _const_spec((d, d)),
            _const_spec((1, d)),
            _const_spec((d, dff)),
            _const_spec((d, dff)),
            _const_spec((dff, d)),
            _const_spec((1, d)),
        ],
        out_specs=row_spec,
        out_shape=jax.ShapeDtypeStruct((rows, d), F32),
        compiler_params=pltpu.CompilerParams(
            dimension_semantics=("parallel",), vmem_limit_bytes=vmem),
        name="ffn_out",
    )(h1, ag, sg, wout, g2, w1, w3, w2, gf)


def _attn_kernel(sink_ref, qt_ref, kc_ref, kp_ref, km_ref, vtc_ref, vtp_ref, vtm_ref, gate_ref, o_ref,
                 *, tq, n_kv):
    nsub = tq // WINDOW
    first_block = pl.program_id(1) * nsub
    lanes4 = Q_PER_KV * WINDOW
    kj = lax.broadcasted_iota(jnp.int32, (WINDOW, lanes4), 0)
    qi = lax.broadcasted_iota(jnp.int32, (WINDOW, lanes4), 1) % WINDOW
    cur_ok = kj <= qi
    cur_f = cur_ok.astype(F32)
    pair_w = 2 * HEAD_DIM

    def cols_of(j):
        return pl.ds(j * WINDOW, WINDOW)

    def scores(j, kh):
        if j == 0:
            k_prev = kp_ref[...]
        else:
            k_prev = kc_ref[cols_of(j - 1), :]
        k_all = jnp.concatenate([kc_ref[cols_of(j), :], k_prev, km_ref[...]], axis=0)
        qt4 = jnp.concatenate(
            [qt_ref[pl.ds((kh * Q_PER_KV + g) * HEAD_DIM, HEAD_DIM), cols_of(j)] for g in range(Q_PER_KV)],
            axis=1)
        zeros = jnp.zeros_like(qt4)
        w = jnp.concatenate([qt4, zeros] if kh % 2 == 0 else [zeros, qt4], axis=0)
        pair = kh // 2
        return _dot(k_all[:, pair * pair_w:(pair + 1) * pair_w], w)

    tasks = [(j, kh) for j in range(nsub) for kh in range(n_kv)]
    s_next = scores(*tasks[0])
    head_out = []
    for idx, (j, kh) in enumerate(tasks):
        s_all = s_next
        if idx + 1 < len(tasks):
            s_next = scores(*tasks[idx + 1])
        s_prev = s_all[WINDOW:2 * WINDOW]
        if j == 0:
            s_prev = s_prev + jnp.where(first_block > 0, 0.0, NEG_INF).astype(F32)
        s = jnp.where(cur_ok, s_all[:WINDOW], s_prev)
        s_meta = s_all[2 * WINDOW:]
        sink = jnp.concatenate(
            [jnp.full((1, WINDOW), sink_ref[kh * Q_PER_KV + g] * LOG2E, F32) for g in range(Q_PER_KV)],
            axis=1)
        m = jnp.maximum(jnp.maximum(jnp.max(s, axis=0, keepdims=True),
                                    jnp.max(s_meta, axis=0, keepdims=True)), sink)
        p = jnp.exp2(s - m)
        pm = jnp.exp2(s_meta - m)
        denom = (jnp.sum(p, axis=0, keepdims=True) + jnp.sum(pm, axis=0, keepdims=True)
                 + jnp.exp2(sink - m))
        p_cur = p * cur_f
        p_prev = p - p_cur
        hs = slice(kh * HEAD_DIM, (kh + 1) * HEAD_DIM)
        if j == 0:
            vt_prev = vtp_ref[hs, :]
        else:
            vt_prev = vtc_ref[hs, cols_of(j - 1)]
        ot = (_dot(vtc_ref[hs, cols_of(j)], p_cur.astype(BF16))
              + _dot(vt_prev, p_prev.astype(BF16))
              + _dot(vtm_ref[hs, :], pm.astype(BF16)))
        ot = ot * (1.0 / denom)
        for g in range(Q_PER_KV):
            head_out.append(ot[:, g * WINDOW:(g + 1) * WINDOW])
        if kh == n_kv - 1:
            attn = jnp.concatenate(head_out, axis=0).T
            o_ref[cols_of(j), :] = (attn * gate_ref[cols_of(j), :].astype(F32)).astype(BF16)
            head_out = []


def _attention(sinks, qt, k, vt, k_meta, vt_meta, gate, *, batch, seq, n_kv, tq, vmem):
    d, rows = qt.shape
    kvw = n_kv * HEAD_DIM
    tq = min(tq, seq)
    nsub = tq // WINDOW
    steps = seq // tq
    blocks = seq // WINDOW

    def prev_block(b, i):
        return jnp.maximum(b * blocks + i * nsub - 1, 0)

    return pl.pallas_call(
        functools.partial(_attn_kernel, tq=tq, n_kv=n_kv),
        grid=(batch, steps),
        in_specs=[
            pl.BlockSpec(memory_space=pltpu.SMEM),
            pl.BlockSpec((d, tq), lambda b, i: (0, b * steps + i)),
            pl.BlockSpec((tq, kvw), lambda b, i: (b * steps + i, 0)),
            pl.BlockSpec((WINDOW, kvw), lambda b, i: (prev_block(b, i), 0)),
            _const_spec((N_META, kvw)),
            pl.BlockSpec((kvw, tq), lambda b, i: (0, b * steps + i)),
            pl.BlockSpec((kvw, WINDOW), lambda b, i: (0, prev_block(b, i))),
            _const_spec((kvw, N_META)),
            pl.BlockSpec((tq, d), lambda b, i: (b * steps + i, 0)),
        ],
        out_specs=pl.BlockSpec((tq, d), lambda b, i: (b * steps + i, 0)),
        out_shape=jax.ShapeDtypeStruct((rows, d), BF16),
        compiler_params=pltpu.CompilerParams(
            dimension_semantics=("parallel", "parallel"), vmem_limit_bytes=vmem),
        name="attention",
    )(sinks, qt, k, k, k_meta, vt, vt, vt_meta, gate)


def _ssm_prep_kernel(ar_ref, ai_ref, ls_ref, br_ref, bi_ref, lr_ref, li_ref, bbr_ref, bbi_ref):
    ar = ar_ref[...]
    ai = ai_ref[...]
    step = jnp.exp(ls_ref[...])
    mag = jnp.exp(ar * step)
    ang = ai * step
    lam_re = mag * jnp.cos(ang)
    lam_im = mag * jnp.sin(ang)
    den = ar * ar + ai * ai
    nr = lam_re - 1.0
    ni = lam_im
    coef_re = (nr * ar + ni * ai) / den
    coef_im = (ni * ar - nr * ai) / den
    lr_ref[...] = lam_re
    li_ref[...] = lam_im
    br = br_ref[...]
    bi = bi_ref[...]
    cr = coef_re[:, None, :]
    ci = coef_im[:, None, :]
    bbr_ref[...] = cr * br - ci * bi
    bbi_ref[...] = cr * bi + ci * br


def _ssm_prep(a_re, a_im, log_step, b_re_t, b_im_t):
    g, n = a_re.shape
    c = b_re_t.shape[1]
    return pl.pallas_call(
        _ssm_prep_kernel,
        out_shape=[
            jax.ShapeDtypeStruct((g, n), F32),
            jax.ShapeDtypeStruct((g, n), F32),
            jax.ShapeDtypeStruct((g, c, n), F32),
            jax.ShapeDtypeStruct((g, c, n), F32),
        ],
        name="ssm_prep",
    )(a_re, a_im, log_step, b_re_t, b_im_t)


def _meta_state_kernel(u_ref, wb0_ref, wb1_ref, lr_ref, li_ref, x0_ref, *, half):
    u = u_ref[...]
    hw = u.shape[1] // 2
    bu0 = _dot(u[:, :hw], wb0_ref[...])
    bu1 = _dot(u[:, hw:], wb1_ref[...])
    bre = jnp.concatenate([bu0[:, :half], bu1[:, :half]], axis=-1)
    bim = jnp.concatenate([bu0[:, half:], bu1[:, half:]], axis=-1)
    lr = lr_ref[...]
    li = li_ref[...]
    xr = jnp.zeros_like(lr)
    xi = jnp.zeros_like(li)
    for t in range(u.shape[0]):
        nr = lr * xr - li * xi + bre[t:t + 1, :]
        ni = lr * xi + li * xr + bim[t:t + 1, :]
        xr, xi = nr, ni
    x0_ref[...] = jnp.concatenate([xr, xi], axis=-1)


def _meta_state(u_meta, wb0, wb1, lam_re_row, lam_im_row):
    nstate = lam_re_row.shape[1]
    return pl.pallas_call(
        functools.partial(_meta_state_kernel, half=nstate // 2),
        out_shape=jax.ShapeDtypeStruct((1, 2 * nstate), F32),
        name="meta_state",
    )(u_meta, wb0, wb1, lam_re_row, lam_im_row)


def _gelu_tanh(y):
    c = math.sqrt(2.0 / math.pi)
    return 0.5 * y * (1.0 + jnp.tanh(c * (y + 0.044715 * (y * y * y))))


def _ssm_kernel(u_ref, gate_ref, x0_ref, wb0_ref, wb1_ref, lr_ref, li_ref,
                wcr0_ref, wci0_ref, wcr1_ref, wci1_ref, dsk_ref, ga_ref, gb_ref,
                o_ref, bu_ref, xs_ref, up_ref, ybm_ref, st_ref, *, tc, nstate, lane_chunk):
    nb = SSM_BATCH_ROWS
    half = nstate // 2
    width = u_ref.shape[2]
    hw = width // 2

    @pl.when(pl.program_id(1) == 0)
    def _():
        st_ref[...] = jnp.broadcast_to(x0_ref[...], st_ref.shape)

    ri = lax.broadcasted_iota(jnp.int32, (PERM_ROWS, PERM_ROWS), 0)
    ci = lax.broadcasted_iota(jnp.int32, (PERM_ROWS, PERM_ROWS), 1)
    src_of_row = (ri % nb) * V7X_BF16_ROWS + ri // nb
    perm = (ci == src_of_row).astype(BF16)
    dst_of_row = (ri % V7X_BF16_ROWS) * nb + ri // V7X_BF16_ROWS
    perm_t = (ci == dst_of_row).astype(BF16)

    nslice = ybm_ref.shape[0]
    ts = tc // nslice
    rs = nb * ts
    d = o_ref.shape[2]

    def project_in(s):
        rows = pl.ds(s * rs, rs)
        for k in range(ts // V7X_BF16_ROWS):
            t0 = s * ts + k * V7X_BF16_ROWS
            xk = jnp.concatenate(
                [u_ref[b, pl.ds(t0, V7X_BF16_ROWS), :] for b in range(nb)], axis=0)
            up_ref[pl.ds(s * rs + k * PERM_ROWS, PERM_ROWS), :] = _dot(perm, xk).astype(BF16)
        up = up_ref[rows, :]
        bu0 = _dot(up[:, :hw], wb0_ref[...])
        bu_ref[rows, 0:half] = bu0[:, :half]
        bu_ref[rows, nstate:nstate + half] = bu0[:, half:]
        bu1 = _dot(up[:, hw:], wb1_ref[...])
        bu_ref[rows, half:nstate] = bu1[:, :half]
        bu_ref[rows, nstate + half:] = bu1[:, half:]

    def scan(s):
        for c in range(nstate // lane_chunk):
            re_l = pl.ds(c * lane_chunk, lane_chunk)
            im_l = pl.ds(nstate + c * lane_chunk, lane_chunk)
            lr = jnp.broadcast_to(lr_ref[:, re_l], (nb, lane_chunk))
            li = jnp.broadcast_to(li_ref[:, re_l], (nb, lane_chunk))
            xr = st_ref[:, re_l]
            xi = st_ref[:, im_l]
            for t in range(s * ts, (s + 1) * ts):
                r = pl.ds(t * nb, nb)
                nr = lr * xr - li * xi + bu_ref[r, re_l]
                ni = lr * xi + li * xr + bu_ref[r, im_l]
                xs_ref[r, re_l] = nr
                xs_ref[r, im_l] = ni
                xr, xi = nr, ni
            st_ref[:, re_l] = xr
            st_ref[:, im_l] = xi

    def readout(s):
        rows = pl.ds(s * rs, rs)
        y0 = (_dot(xs_ref[rows, 0:half].astype(BF16), wcr0_ref[...])
              + _dot(xs_ref[rows, nstate:nstate + half].astype(BF16), wci0_ref[...]))
        y1 = (_dot(xs_ref[rows, half:nstate].astype(BF16), wcr1_ref[...])
              + _dot(xs_ref[rows, nstate + half:].astype(BF16), wci1_ref[...]))
        return jnp.concatenate([y0, y1], axis=-1) + dsk_ref[...] * up_ref[rows, :].astype(F32)

    def activate(s, y):
        yg = _gelu_tanh(y).astype(BF16)
        for k in range(ts // V7X_BF16_ROWS):
            yk = _dot(perm_t, yg[k * PERM_ROWS:(k + 1) * PERM_ROWS, :]).astype(BF16)
            for b in range(nb):
                ybm_ref[s, pl.ds(b * ts + k * V7X_BF16_ROWS, V7X_BF16_ROWS), :] = (
                    yk[b * V7X_BF16_ROWS:(b + 1) * V7X_BF16_ROWS, :])

    def glu(s):
        ybm = ybm_ref[s]
        return _dot(ybm, ga_ref[...]), _dot(ybm, gb_ref[...])

    def emit(s, a, g):
        t_sl = pl.ds(s * ts, ts)
        gate = gate_ref[:, t_sl, :].astype(F32).reshape(rs, d)
        o_ref[:, t_sl, :] = (a * _sigmoid(g) * gate).reshape(nb, ts, d).astype(BF16)

    project_in(0)
    if nslice > 1:
        project_in(1)
    scan(0)
    for s in range(nslice):
        if s + 2 < nslice:
            project_in(s + 2)
        y = readout(s)
        if s + 1 < nslice:
            scan(s + 1)
        activate(s, y)
        a, g = glu(s)
        emit(s, a, g)


def _ssm(u3, gate3, x0, wb0, wb1, lam_re_row, lam_im_row, wcr0, wci0, wcr1, wci1, dsk, ga, gb,
         *, tc, lane_chunk, vmem):
    batch, seq, width = u3.shape
    d = gate3.shape[2]
    nstate = lam_re_row.shape[1]
    nb = SSM_BATCH_ROWS
    tc = min(tc, seq)
    consts = [x0, wb0, wb1, lam_re_row, lam_im_row, wcr0, wci0, wcr1, wci1, dsk, ga, gb]
    return pl.pallas_call(
        functools.partial(_ssm_kernel, tc=tc, nstate=nstate, lane_chunk=lane_chunk),
        grid=(batch // nb, seq // tc),
        in_specs=[
            pl.BlockSpec((nb, tc, width), lambda g, t: (g, t, 0)),
            pl.BlockSpec((nb, tc, d), lambda g, t: (g, t, 0)),
        ] + [_const_spec(c.shape) for c in consts],
        out_specs=pl.BlockSpec((nb, tc, d), lambda g, t: (g, t, 0)),
        out_shape=jax.ShapeDtypeStruct((batch, seq, d), BF16),
        scratch_shapes=[
            pltpu.VMEM((nb * tc, 2 * nstate), F32),
            pltpu.VMEM((nb * tc, 2 * nstate), F32),
            pltpu.VMEM((nb * tc, width), BF16),
            pltpu.VMEM((SSM_OUT_SLICES, nb * tc // SSM_OUT_SLICES, width), BF16),
            pltpu.VMEM((nb, 2 * nstate), F32),
        ],
        compiler_params=pltpu.CompilerParams(
            dimension_semantics=("parallel", "arbitrary"), vmem_limit_bytes=vmem),
        name="ssm",
    )(u3, gate3, *consts)


def _block_diag(blocks):
    g, r, c = blocks.shape
    eye = jnp.eye(g, dtype=blocks.dtype)
    return jnp.einsum("grc,gh->grhc", blocks, eye).reshape(g * r, g * c)


def kernel(x, meta_tokens, ffn1_norm, ffn1_w1, ffn1_w3, ffn1_w2, mix_norm, w_in, attn_sinks, ssm_a_re, ssm_a_im, ssm_log_step, ssm_b_re, ssm_b_im, ssm_c_re, ssm_c_im, ssm_d, ssm_glu_a, ssm_glu_b, w_out, ffn2_norm, ffn2_w1, ffn2_w3, ffn2_w2, final_norm):
    batch, seq, d = x.shape
    n_q = d // HEAD_DIM
    n_kv = n_q // Q_PER_KV
    kvw = n_kv * HEAD_DIM
    width = d // 2
    groups = width // SSM_GROUP
    nstate = groups * SSM_STATE
    half = nstate // 2
    hw = width // 2
    assert seq % WINDOW == 0 and batch % SSM_BATCH_ROWS == 0
    vmem = V7X_VMEM_BYTES - 8 * 1024 * 1024

    bf = lambda w: w.astype(BF16)
    row = lambda v: v.reshape(1, -1).astype(F32)

    lam_re, lam_im, bb_re_t, bb_im_t = _ssm_prep(
        ssm_a_re[0].astype(F32), ssm_a_im[0].astype(F32), ssm_log_step[0].astype(F32).reshape(groups, 1),
        jnp.swapaxes(ssm_b_re[0].astype(F32), 1, 2), jnp.swapaxes(ssm_b_im[0].astype(F32), 1, 2))
    bd_re = _block_diag(bb_re_t)
    bd_im = _block_diag(bb_im_t)
    wb0 = bf(jnp.concatenate([bd_re[:hw, :half], bd_im[:hw, :half]], axis=1))
    wb1 = bf(jnp.concatenate([bd_re[hw:, half:], bd_im[hw:, half:]], axis=1))
    cd_re = _block_diag(jnp.swapaxes(ssm_c_re[0].astype(F32), 1, 2))
    cd_im = _block_diag(jnp.swapaxes(-ssm_c_im[0].astype(F32), 1, 2))
    wcr0, wci0 = bf(cd_re[:half, :hw]), bf(cd_im[:half, :hw])
    wcr1, wci1 = bf(cd_re[half:, hw:]), bf(cd_im[half:, hw:])
    lam_re_row = lam_re.reshape(1, nstate)
    lam_im_row = lam_im.reshape(1, nstate)

    w1a, w3a, w2a = bf(ffn1_w1[0]), bf(ffn1_w3[0]), bf(ffn1_w2[0])
    g1, gm = row(ffn1_norm[0]), row(mix_norm[0])
    win = w_in[0]
    c_k, c_v, c_u = d, d + kvw, d + 2 * kvw
    wcol = bf(jnp.concatenate([win[:, :c_k] * (HEAD_DIM ** -0.5 * LOG2E), win[:, c_v:c_u]], axis=1).T)
    wrow = bf(jnp.concatenate([win[:, c_k:c_v], win[:, c_u:]], axis=1))
    ffn1 = functools.partial(_ffn1, g1=g1, w1=w1a, w3=w3a, w2=w2a, vmem=vmem)
    proj = functools.partial(_proj, gm=gm, wrow=wrow, wcol=wcol, d_kv=kvw, d_u=width, vmem=vmem)

    _, vt_meta, k_meta, u_meta, _, _ = proj(ffn1(meta_tokens.astype(F32), tm=N_META), tm=N_META)
    x0 = _meta_state(u_meta, wb0, wb1, lam_re_row, lam_im_row)

    rows = batch * seq
    h1 = ffn1(x.reshape(rows, d), tm=512)
    qt, vt, k, u, gate_a, gate_s = proj(h1, tm=512)

    attn_g = _attention(attn_sinks[0].astype(F32), qt, k, vt, k_meta, vt_meta, gate_a,
                        batch=batch, seq=seq, n_kv=n_kv, tq=512, vmem=vmem)
    ssm_g = _ssm(u.reshape(batch, seq, width), gate_s.reshape(batch, seq, d), x0, wb0, wb1,
                 lam_re_row, lam_im_row, wcr0, wci0, wcr1, wci1, row(ssm_d[0]),
                 bf(ssm_glu_a[0]), bf(ssm_glu_b[0]), tc=128, lane_chunk=512, vmem=vmem)

    out = _ffn_out(h1, attn_g, ssm_g.reshape(rows, d), bf(w_out[0]), row(ffn2_norm[0]),
                   bf(ffn2_w1[0]), bf(ffn2_w3[0]), bf(ffn2_w2[0]), row(final_norm), tm=512, vmem=vmem)
    return out.reshape(batch, seq, d)
```

```python
import functools
import math

import jax
import jax.numpy as jnp
from jax import lax
from jax.experimental import pallas as pl
from jax.experimental.pallas import tpu as pltpu

F32 = jnp.float32
BF16 = jnp.bfloat16

N_META = 16
HEAD_DIM = 64
Q_PER_KV = 4
WINDOW = 128
SSM_GROUP = 16
SSM_STATE = 64
NORM_EPS = 1e-6
NEG_INF = -1e30
LOG2E = math.log2(math.e)

V7X_VMEM_BYTES = 64 * 1024 * 1024
V7X_SUBLANES = 8
V7X_BF16_ROWS = 16
SSM_BATCH_ROWS = V7X_SUBLANES
PERM_ROWS = V7X_BF16_ROWS * SSM_BATCH_ROWS
SSM_OUT_SLICES = 2

def _const_spec(shape):
    nd = len(shape)
    return pl.BlockSpec(shape, lambda *_: (0,) * nd, pipeline_mode=pl.Buffered(1))


def _dot(a, b):
    return jnp.dot(a, b, preferred_element_type=F32)


def _dot_nt(a, b):
    return lax.dot_general(a, b, (((1,), (1,)), ((), ())), preferred_element_type=F32)


def _rms(x, g):
    return x * lax.rsqrt(jnp.mean(x * x, axis=-1, keepdims=True) + NORM_EPS) * g


def _sigmoid(x):
    return 0.5 * jnp.tanh(0.5 * x) + 0.5


def _swiglu(xn_bf16, w1_ref, w3_ref, w2_ref):
    a = _dot(xn_bf16, w1_ref[...])
    b = _dot(xn_bf16, w3_ref[...])
    act = (a * _sigmoid(a) * b).astype(BF16)
    return _dot(act, w2_ref[...])


def _ffn1_kernel(x_ref, g1_ref, w1_ref, w3_ref, w2_ref, h_ref):
    x = x_ref[...]
    h_ref[...] = x + 0.5 * _swiglu(_rms(x, g1_ref[...]).astype(BF16), w1_ref, w3_ref, w2_ref)


def _ffn1(x2d, g1, w1, w3, w2, *, tm, vmem):
    rows, d = x2d.shape
    dff = w1.shape[1]
    tm = min(tm, rows)
    tok = pl.BlockSpec((tm, d), lambda i: (i, 0))
    return pl.pallas_call(
        _ffn1_kernel,
        grid=(rows // tm,),
        in_specs=[tok, _const_spec((1, d)), _const_spec((d, dff)), _const_spec((d, dff)),
                  _const_spec((dff, d))],
        out_specs=tok,
        out_shape=jax.ShapeDtypeStruct((rows, d), F32),
        compiler_params=pltpu.CompilerParams(
            dimension_semantics=("parallel",), vmem_limit_bytes=vmem),
        name="ffn1",
    )(x2d, g1, w1, w3, w2)


def _proj_kernel(h_ref, gm_ref, wrow_ref, wcol_ref, qt_ref, vt_ref, k_ref, u_ref, ga_ref, gs_ref):
    hn = _rms(h_ref[...], gm_ref[...]).astype(BF16)
    pr = _dot(hn, wrow_ref[...])
    c0 = 0
    for ref, is_gate in ((k_ref, False), (u_ref, False), (ga_ref, True), (gs_ref, True)):
        c1 = c0 + ref.shape[1]
        piece = pr[:, c0:c1]
        ref[...] = (_sigmoid(piece) if is_gate else piece).astype(BF16)
        c0 = c1
    pt = _dot_nt(wcol_ref[...], hn)
    nq = qt_ref.shape[0]
    qt_ref[...] = pt[:nq, :].astype(BF16)
    vt_ref[...] = pt[nq:, :].astype(BF16)


def _proj(h2d, gm, wrow, wcol, *, d_kv, d_u, tm, vmem):
    rows, d = h2d.shape
    tm = min(tm, rows)
    tok = lambda w: pl.BlockSpec((tm, w), lambda i: (i, 0))
    feat = lambda w: pl.BlockSpec((w, tm), lambda i: (0, i))
    return pl.pallas_call(
        _proj_kernel,
        grid=(rows // tm,),
        in_specs=[
            tok(d),
            _const_spec((1, d)),
            _const_spec(wrow.shape),
            _const_spec(wcol.shape),
        ],
        out_specs=[feat(d), feat(d_kv), tok(d_kv), tok(d_u), tok(d), tok(d)],
        out_shape=[
            jax.ShapeDtypeStruct((d, rows), BF16),
            jax.ShapeDtypeStruct((d_kv, rows), BF16),
            jax.ShapeDtypeStruct((rows, d_kv), BF16),
            jax.ShapeDtypeStruct((rows, d_u), BF16),
            jax.ShapeDtypeStruct((rows, d), BF16),
            jax.ShapeDtypeStruct((rows, d), BF16),
        ],
        compiler_params=pltpu.CompilerParams(
            dimension_semantics=("parallel",), vmem_limit_bytes=vmem),
        name="proj",
    )(h2d, gm, wrow, wcol)


def _ffn_out_kernel(h_ref, ag_ref, sg_ref, wout_ref, g2_ref, w1_ref, w3_ref, w2_ref, gf_ref, o_ref):
    merged = (ag_ref[...].astype(F32) + sg_ref[...].astype(F32)).astype(BF16)
    h = h_ref[...] + _dot(merged, wout_ref[...])
    h = h + 0.5 * _swiglu(_rms(h, g2_ref[...]).astype(BF16), w1_ref, w3_ref, w2_ref)
    o_ref[...] = _rms(h, gf_ref[...])


def _ffn_out(h1, ag, sg, wout, g2, w1, w3, w2, gf, *, tm, vmem):
    rows, d = h1.shape
    dff = w1.shape[1]
    tm = min(tm, rows)
    row_spec = pl.BlockSpec((tm, d), lambda i: (i, 0))
    return pl.pallas_call(
        _ffn_out_kernel,
        grid=(rows // tm,),
        in_specs=[
            row_spec, row_spec, row_spec,
            _const_spec((d, d)),
            _const_spec((1, d)),
            _const_spec((d, dff)),
            _const_spec((d, dff)),
            _const_spec((dff, d)),
            _const_spec((1, d)),
        ],
        out_specs=row_spec,
        out_shape=jax.ShapeDtypeStruct((rows, d), F32),
        compiler_params=pltpu.CompilerParams(
            dimension_semantics=("parallel",), vmem_limit_bytes=vmem),
        name="ffn_out",
    )(h1, ag, sg, wout, g2, w1, w3, w2, gf)


def _attn_kernel(sink_ref, qt_ref, kc_ref, kp_ref, km_ref, vtc_ref, vtp_ref, vtm_ref, gate_ref, o_ref,
                 *, tq, n_kv):
    nsub = tq // WINDOW
    first_block = pl.program_id(1) * nsub
    lanes4 = Q_PER_KV * WINDOW
    kj = lax.broadcasted_iota(jnp.int32, (WINDOW, lanes4), 0)
    qi = lax.broadcasted_iota(jnp.int32, (WINDOW, lanes4), 1) % WINDOW
    cur_ok = kj <= qi
    cur_f = cur_ok.astype(F32)
    pair_w = 2 * HEAD_DIM

    def cols_of(j):
        return pl.ds(j * WINDOW, WINDOW)

    def scores(j, kh):
        if j == 0:
            k_prev = kp_ref[...]
        else:
            k_prev = kc_ref[cols_of(j - 1), :]
        k_all = jnp.concatenate([kc_ref[cols_of(j), :], k_prev, km_ref[...]], axis=0)
        qt4 = jnp.concatenate(
            [qt_ref[pl.ds((kh * Q_PER_KV + g) * HEAD_DIM, HEAD_DIM), cols_of(j)] for g in range(Q_PER_KV)],
            axis=1)
        zeros = jnp.zeros_like(qt4)
        w = jnp.concatenate([qt4, zeros] if kh % 2 == 0 else [zeros, qt4], axis=0)
        pair = kh // 2
        return _dot(k_all[:, pair * pair_w:(pair + 1) * pair_w], w)

    tasks = [(j, kh) for j in range(nsub) for kh in range(n_kv)]
    s_next = scores(*tasks[0])
    head_out = []
    for idx, (j, kh) in enumerate(tasks):
        s_all = s_next
        if idx + 1 < len(tasks):
            s_next = scores(*tasks[idx + 1])
        s_prev = s_all[WINDOW:2 * WINDOW]
        if j == 0:
            s_prev = s_prev + jnp.where(first_block > 0, 0.0, NEG_INF).astype(F32)
        s = jnp.where(cur_ok, s_all[:WINDOW], s_prev)
        s_meta = s_all[2 * WINDOW:]
        sink = jnp.concatenate(
            [jnp.full((1, WINDOW), sink_ref[kh * Q_PER_KV + g] * LOG2E, F32) for g in range(Q_PER_KV)],
            axis=1)
        m = jnp.maximum(jnp.maximum(jnp.max(s, axis=0, keepdims=True),
                                    jnp.max(s_meta, axis=0, keepdims=True)), sink)
        p = jnp.exp2(s - m)
        pm = jnp.exp2(s_meta - m)
        denom = (jnp.sum(p, axis=0, keepdims=True) + jnp.sum(pm, axis=0, keepdims=True)
                 + jnp.exp2(sink - m))
        p_cur = p * cur_f
        p_prev = p - p_cur
        hs = slice(kh * HEAD_DIM, (kh + 1) * HEAD_DIM)
        if j == 0:
            vt_prev = vtp_ref[hs, :]
        else:
            vt_prev = vtc_ref[hs, cols_of(j - 1)]
        vt_band = jnp.concatenate([vtc_ref[hs, cols_of(j)], vt_prev], axis=1)
        p_band = jnp.concatenate([p_cur.astype(BF16), p_prev.astype(BF16)], axis=0)
        ot = _dot(vt_band, p_band) + _dot(vtm_ref[hs, :], pm.astype(BF16))
        ot = ot * (1.0 / denom)
        for g in range(Q_PER_KV):
            head_out.append(ot[:, g * WINDOW:(g + 1) * WINDOW])
        if kh == n_kv - 1:
            attn = jnp.concatenate(head_out, axis=0).T
            o_ref[cols_of(j), :] = (attn * gate_ref[cols_of(j), :].astype(F32)).astype(BF16)
            head_out = []


def _attention(sinks, qt, k, vt, k_meta, vt_meta, gate, *, batch, seq, n_kv, tq, vmem):
    d, rows = qt.shape
    kvw = n_kv * HEAD_DIM
    tq = min(tq, seq)
    nsub = tq // WINDOW
    steps = seq // tq
    blocks = seq // WINDOW

    def prev_block(b, i):
        return jnp.maximum(b * blocks + i * nsub - 1, 0)

    return pl.pallas_call(
        functools.partial(_attn_kernel, tq=tq, n_kv=n_kv),
        grid=(batch, steps),
        in_specs=[
            pl.BlockSpec(memory_space=pltpu.SMEM),
            pl.BlockSpec((d, tq), lambda b, i: (0, b * steps + i)),
            pl.BlockSpec((tq, kvw), lambda b, i: (b * steps + i, 0)),
            pl.BlockSpec((WINDOW, kvw), lambda b, i: (prev_block(b, i), 0)),
            _const_spec((N_META, kvw)),
            pl.BlockSpec((kvw, tq), lambda b, i: (0, b * steps + i)),
            pl.BlockSpec((kvw, WINDOW), lambda b, i: (0, prev_block(b, i))),
            _const_spec((kvw, N_META)),
            pl.BlockSpec((tq, d), lambda b, i: (b * steps + i, 0)),
        ],
        out_specs=pl.BlockSpec((tq, d), lambda b, i: (b * steps + i, 0)),
        out_shape=jax.ShapeDtypeStruct((rows, d), BF16),
        compiler_params=pltpu.CompilerParams(
            dimension_semantics=("parallel", "parallel"), vmem_limit_bytes=vmem),
        name="attention",
    )(sinks, qt, k, k, k_meta, vt, vt, vt_meta, gate)


def _ssm_prep_kernel(ar_ref, ai_ref, ls_ref, br_ref, bi_ref, lr_ref, li_ref, bbr_ref, bbi_ref):
    ar = ar_ref[...]
    ai = ai_ref[...]
    step = jnp.exp(ls_ref[...])
    mag = jnp.exp(ar * step)
    ang = ai * step
    lam_re = mag * jnp.cos(ang)
    lam_im = mag * jnp.sin(ang)
    den = ar * ar + ai * ai
    nr = lam_re - 1.0
    ni = lam_im
    coef_re = (nr * ar + ni * ai) / den
    coef_im = (ni * ar - nr * ai) / den
    lr_ref[...] = lam_re
    li_ref[...] = lam_im
    br = br_ref[...]
    bi = bi_ref[...]
    cr = coef_re[:, None, :]
    ci = coef_im[:, None, :]
    bbr_ref[...] = cr * br - ci * bi
    bbi_ref[...] = cr * bi + ci * br


def _ssm_prep(a_re, a_im, log_step, b_re_t, b_im_t):
    g, n = a_re.shape
    c = b_re_t.shape[1]
    return pl.pallas_call(
        _ssm_prep_kernel,
        out_shape=[
            jax.ShapeDtypeStruct((g, n), F32),
            jax.ShapeDtypeStruct((g, n), F32),
            jax.ShapeDtypeStruct((g, c, n), F32),
            jax.ShapeDtypeStruct((g, c, n), F32),
        ],
        name="ssm_prep",
    )(a_re, a_im, log_step, b_re_t, b_im_t)


def _meta_state_kernel(u_ref, wb0_ref, wb1_ref, lr_ref, li_ref, x0_ref, *, half):
    u = u_ref[...]
    hw = u.shape[1] // 2
    bu0 = _dot(u[:, :hw], wb0_ref[...])
    bu1 = _dot(u[:, hw:], wb1_ref[...])
    bre = jnp.concatenate([bu0[:, :half], bu1[:, :half]], axis=-1)
    bim = jnp.concatenate([bu0[:, half:], bu1[:, half:]], axis=-1)
    lr = lr_ref[...]
    li = li_ref[...]
    xr = jnp.zeros_like(lr)
    xi = jnp.zeros_like(li)
    for t in range(u.shape[0]):
        nr = lr * xr - li * xi + bre[t:t + 1, :]
        ni = lr * xi + li * xr + bim[t:t + 1, :]
        xr, xi = nr, ni
    x0_ref[...] = jnp.concatenate([xr, xi], axis=-1)


def _meta_state(u_meta, wb0, wb1, lam_re_row, lam_im_row):
    nstate = lam_re_row.shape[1]
    return pl.pallas_call(
        functools.partial(_meta_state_kernel, half=nstate // 2),
        out_shape=jax.ShapeDtypeStruct((1, 2 * nstate), F32),
        name="meta_state",
    )(u_meta, wb0, wb1, lam_re_row, lam_im_row)


def _gelu_tanh(y):
    c = math.sqrt(2.0 / math.pi)
    return 0.5 * y * (1.0 + jnp.tanh(c * (y + 0.044715 * (y * y * y))))


def _ssm_kernel(u_ref, gate_ref, x0_ref, wb0_ref, wb1_ref, lr_ref, li_ref,
                wcr0_ref, wci0_ref, wcr1_ref, wci1_ref, dsk_ref, ga_ref, gb_ref,
                o_ref, bu_ref, xs_ref, up_ref, ybm_ref, st_ref, *, tc, nstate, lane_chunk):
    nb = SSM_BATCH_ROWS
    half = nstate // 2
    width = u_ref.shape[2]
    hw = width // 2

    @pl.when(pl.program_id(1) == 0)
    def _():
        st_ref[...] = jnp.broadcast_to(x0_ref[...], st_ref.shape)

    ri = lax.broadcasted_iota(jnp.int32, (PERM_ROWS, PERM_ROWS), 0)
    ci = lax.broadcasted_iota(jnp.int32, (PERM_ROWS, PERM_ROWS), 1)
    src_of_row = (ri % nb) * V7X_BF16_ROWS + ri // nb
    perm = (ci == src_of_row).astype(BF16)
    dst_of_row = (ri % V7X_BF16_ROWS) * nb + ri // V7X_BF16_ROWS
    perm_t = (ci == dst_of_row).astype(BF16)

    nslice = ybm_ref.shape[0]
    ts = tc // nslice
    rs = nb * ts
    d = o_ref.shape[2]

    def project_in(s):
        rows = pl.ds(s * rs, rs)
        for k in range(ts // V7X_BF16_ROWS):
            t0 = s * ts + k * V7X_BF16_ROWS
            xk = jnp.concatenate(
                [u_ref[b, pl.ds(t0, V7X_BF16_ROWS), :] for b in range(nb)], axis=0)
            up_ref[pl.ds(s * rs + k * PERM_ROWS, PERM_ROWS), :] = _dot(perm, xk).astype(BF16)
        up = up_ref[rows, :]
        bu0 = _dot(up[:, :hw], wb0_ref[...])
        bu_ref[rows, 0:half] = bu0[:, :half]
        bu_ref[rows, nstate:nstate + half] = bu0[:, half:]
        bu1 = _dot(up[:, hw:], wb1_ref[...])
        bu_ref[rows, half:nstate] = bu1[:, :half]
        bu_ref[rows, nstate + half:] = bu1[:, half:]

    def scan(s):
        for c in range(nstate // lane_chunk):
            re_l = pl.ds(c * lane_chunk, lane_chunk)
            im_l = pl.ds(nstate + c * lane_chunk, lane_chunk)
            lr = jnp.broadcast_to(lr_ref[:, re_l], (nb, lane_chunk))
            li = jnp.broadcast_to(li_ref[:, re_l], (nb, lane_chunk))
            xr = st_ref[:, re_l]
            xi = st_ref[:, im_l]
            for t in range(s * ts, (s + 1) * ts):
                r = pl.ds(t * nb, nb)
                nr = lr * xr - li * xi + bu_ref[r, re_l]
                ni = lr * xi + li * xr + bu_ref[r, im_l]
                xs_ref[r, re_l] = nr
                xs_ref[r, im_l] = ni
                xr, xi = nr, ni
            st_ref[:, re_l] = xr
            st_ref[:, im_l] = xi

    def readout(s):
        rows = pl.ds(s * rs, rs)
        y0 = (_dot(xs_ref[rows, 0:half].astype(BF16), wcr0_ref[...])
              + _dot(xs_ref[rows, nstate:nstate + half].astype(BF16), wci0_ref[...]))
        y1 = (_dot(xs_ref[rows, half:nstate].astype(BF16), wcr1_ref[...])
              + _dot(xs_ref[rows, nstate + half:].astype(BF16), wci1_ref[...]))
        return jnp.concatenate([y0, y1], axis=-1) + dsk_ref[...] * up_ref[rows, :].astype(F32)

    def activate(s, y):
        yg = _gelu_tanh(y).astype(BF16)
        for k in range(ts // V7X_BF16_ROWS):
            yk = _dot(perm_t, yg[k * PERM_ROWS:(k + 1) * PERM_ROWS, :]).astype(BF16)
            for b in range(nb):
                ybm_ref[s, pl.ds(b * ts + k * V7X_BF16_ROWS, V7X_BF16_ROWS), :] = (
                    yk[b * V7X_BF16_ROWS:(b + 1) * V7X_BF16_ROWS, :])

    def glu(s):
        ybm = ybm_ref[s]
        return _dot(ybm, ga_ref[...]), _dot(ybm, gb_ref[...])

    def emit(s, a, g):
        t_sl = pl.ds(s * ts, ts)
        gate = gate_ref[:, t_sl, :].astype(F32).reshape(rs, d)
        o_ref[:, t_sl, :] = (a * _sigmoid(g) * gate).reshape(nb, ts, d).astype(BF16)

    project_in(0)
    for s in range(nslice):
        if s + 1 < nslice:
            project_in(s + 1)
        if s > 0:
            a, g = glu(s - 1)
        scan(s)
        if s > 0:
            emit(s - 1, a, g)
        activate(s, readout(s))
    a, g = glu(nslice - 1)
    emit(nslice - 1, a, g)


def _ssm(u3, gate3, x0, wb0, wb1, lam_re_row, lam_im_row, wcr0, wci0, wcr1, wci1, dsk, ga, gb,
         *, tc, lane_chunk, vmem):
    batch, seq, width = u3.shape
    d = gate3.shape[2]
    nstate = lam_re_row.shape[1]
    nb = SSM_BATCH_ROWS
    tc = min(tc, seq)
    consts = [x0, wb0, wb1, lam_re_row, lam_im_row, wcr0, wci0, wcr1, wci1, dsk, ga, gb]
    return pl.pallas_call(
        functools.partial(_ssm_kernel, tc=tc, nstate=nstate, lane_chunk=lane_chunk),
        grid=(batch // nb, seq // tc),
        in_specs=[
            pl.BlockSpec((nb, tc, width), lambda g, t: (g, t, 0)),
            pl.BlockSpec((nb, tc, d), lambda g, t: (g, t, 0)),
        ] + [_const_spec(c.shape) for c in consts],
        out_specs=pl.BlockSpec((nb, tc, d), lambda g, t: (g, t, 0)),
        out_shape=jax.ShapeDtypeStruct((batch, seq, d), BF16),
        scratch_shapes=[
            pltpu.VMEM((nb * tc, 2 * nstate), F32),
            pltpu.VMEM((nb * tc, 2 * nstate), F32),
            pltpu.VMEM((nb * tc, width), BF16),
            pltpu.VMEM((SSM_OUT_SLICES, nb * tc // SSM_OUT_SLICES, width), BF16),
            pltpu.VMEM((nb, 2 * nstate), F32),
        ],
        compiler_params=pltpu.CompilerParams(
            dimension_semantics=("parallel", "arbitrary"), vmem_limit_bytes=vmem),
        name="ssm",
    )(u3, gate3, *consts)


def _block_diag(blocks):
    g, r, c = blocks.shape
    tiled = jnp.tile(blocks.reshape(g * r, c), (1, g))
    row_group = jnp.arange(g * r, dtype=jnp.int32)[:, None] // r
    col_group = jnp.arange(g * c, dtype=jnp.int32)[None, :] // c
    return jnp.where(row_group == col_group, tiled, 0.0)


def kernel(x, meta_tokens, ffn1_norm, ffn1_w1, ffn1_w3, ffn1_w2, mix_norm, w_in, attn_sinks, ssm_a_re, ssm_a_im, ssm_log_step, ssm_b_re, ssm_b_im, ssm_c_re, ssm_c_im, ssm_d, ssm_glu_a, ssm_glu_b, w_out, ffn2_norm, ffn2_w1, ffn2_w3, ffn2_w2, final_norm):
    batch, seq, d = x.shape
    n_q = d // HEAD_DIM
    n_kv = n_q // Q_PER_KV
    kvw = n_kv * HEAD_DIM
    width = d // 2
    groups = width // SSM_GROUP
    nstate = groups * SSM_STATE
    half = nstate // 2
    hw = width // 2
    assert seq % WINDOW == 0 and batch % SSM_BATCH_ROWS == 0
    vmem = V7X_VMEM_BYTES - 8 * 1024 * 1024

    bf = lambda w: w.astype(BF16)
    row = lambda v: v.reshape(1, -1).astype(F32)

    lam_re, lam_im, bb_re_t, bb_im_t = _ssm_prep(
        ssm_a_re[0].astype(F32), ssm_a_im[0].astype(F32), ssm_log_step[0].astype(F32).reshape(groups, 1),
        jnp.swapaxes(ssm_b_re[0].astype(F32), 1, 2), jnp.swapaxes(ssm_b_im[0].astype(F32), 1, 2))
    bd_re = _block_diag(bb_re_t)
    bd_im = _block_diag(bb_im_t)
    wb0 = bf(jnp.concatenate([bd_re[:hw, :half], bd_im[:hw, :half]], axis=1))
    wb1 = bf(jnp.concatenate([bd_re[hw:, half:], bd_im[hw:, half:]], axis=1))
    cd_re = _block_diag(jnp.swapaxes(ssm_c_re[0].astype(F32), 1, 2))
    cd_im = _block_diag(jnp.swapaxes(-ssm_c_im[0].astype(F32), 1, 2))
    wcr0, wci0 = bf(cd_re[:half, :hw]), bf(cd_im[:half, :hw])
    wcr1, wci1 = bf(cd_re[half:, hw:]), bf(cd_im[half:, hw:])
    lam_re_row = lam_re.reshape(1, nstate)
    lam_im_row = lam_im.reshape(1, nstate)

    w1a, w3a, w2a = bf(ffn1_w1[0]), bf(ffn1_w3[0]), bf(ffn1_w2[0])
    g1, gm = row(ffn1_norm[0]), row(mix_norm[0])
    win = w_in[0]
    c_k, c_v, c_u = d, d + kvw, d + 2 * kvw
    wcol = jnp.concatenate([bf(win[:, :c_k] * (HEAD_DIM ** -0.5 * LOG2E)).T, bf(win[:, c_v:c_u]).T], axis=0)
    wrow = bf(jnp.concatenate([win[:, c_k:c_v], win[:, c_u:]], axis=1))
    ffn1 = functools.partial(_ffn1, g1=g1, w1=w1a, w3=w3a, w2=w2a, vmem=vmem)
    proj = functools.partial(_proj, gm=gm, wrow=wrow, wcol=wcol, d_kv=kvw, d_u=width, vmem=vmem)

    _, vt_meta, k_meta, u_meta, _, _ = proj(ffn1(meta_tokens.astype(F32), tm=N_META), tm=N_META)
    x0 = _meta_state(u_meta, wb0, wb1, lam_re_row, lam_im_row)

    rows = batch * seq
    h1 = ffn1(x.reshape(rows, d), tm=512)
    qt, vt, k, u, gate_a, gate_s = proj(h1, tm=1024)

    attn_g = _attention(attn_sinks[0].astype(F32), qt, k, vt, k_meta, vt_meta, gate_a,
                        batch=batch, seq=seq, n_kv=n_kv, tq=1024, vmem=vmem)
    ssm_g = _ssm(u.reshape(batch, seq, width), gate_s.reshape(batch, seq, d), x0, wb0, wb1,
                 lam_re_row, lam_im_row, wcr0, wci0, wcr1, wci1, row(ssm_d[0]),
                 bf(ssm_glu_a[0]), bf(ssm_glu_b[0]), tc=128, lane_chunk=512, vmem=vmem)

    out = _ffn_out(h1, attn_g, ssm_g.reshape(rows, d), bf(w_out[0]), row(ffn2_norm[0]),
                   bf(ffn2_w1[0]), bf(ffn2_w3[0]), bf(ffn2_w2[0]), row(final_norm), tm=512, vmem=vmem)
    return out.reshape(batch, seq, d)
```

```python
import functools
import math

import jax
import jax.numpy as jnp
from jax import lax
from jax.experimental import pallas as pl
from jax.experimental.pallas import tpu as pltpu

F32 = jnp.float32
BF16 = jnp.bfloat16

N_META = 16
HEAD_DIM = 64
Q_PER_KV = 4
WINDOW = 128
SSM_GROUP = 16
SSM_STATE = 64
NORM_EPS = 1e-6
NEG_INF = -1e30
LOG2E = math.log2(math.e)

V7X_VMEM_BYTES = 64 * 1024 * 1024
V7X_SUBLANES = 8
V7X_BF16_ROWS = 16
SSM_BATCH_ROWS = V7X_SUBLANES
PERM_ROWS = V7X_BF16_ROWS * SSM_BATCH_ROWS
SSM_OUT_SLICES = 2

def _const_spec(shape):
    nd = len(shape)
    return pl.BlockSpec(shape, lambda *_: (0,) * nd, pipeline_mode=pl.Buffered(1))


def _dot(a, b):
    return jnp.dot(a, b, preferred_element_type=F32)


def _dot_nt(a, b):
    return lax.dot_general(a, b, (((1,), (1,)), ((), ())), preferred_element_type=F32)


def _rms(x, g):
    return x * lax.rsqrt(jnp.mean(x * x, axis=-1, keepdims=True) + NORM_EPS) * g


def _sigmoid(x):
    return 0.5 * jnp.tanh(0.5 * x) + 0.5


def _swiglu(xn_bf16, w1_ref, w3_ref, w2_ref):
    a = _dot(xn_bf16, w1_ref[...])
    b = _dot(xn_bf16, w3_ref[...])
    act = (a * _sigmoid(a) * b).astype(BF16)
    return _dot(act, w2_ref[...])


def _ffn1_kernel(x_ref, g1_ref, w1_ref, w3_ref, w2_ref, h_ref):
    x = x_ref[...]
    h_ref[...] = x + 0.5 * _swiglu(_rms(x, g1_ref[...]).astype(BF16), w1_ref, w3_ref, w2_ref)


def _ffn1(x2d, g1, w1, w3, w2, *, tm, vmem):
    rows, d = x2d.shape
    dff = w1.shape[1]
    tm = min(tm, rows)
    tok = pl.BlockSpec((tm, d), lambda i: (i, 0))
    return pl.pallas_call(
        _ffn1_kernel,
        grid=(rows // tm,),
        in_specs=[tok, _const_spec((1, d)), _const_spec((d, dff)), _const_spec((d, dff)),
                  _const_spec((dff, d))],
        out_specs=tok,
        out_shape=jax.ShapeDtypeStruct((rows, d), F32),
        compiler_params=pltpu.CompilerParams(
            dimension_semantics=("parallel",), vmem_limit_bytes=vmem),
        name="ffn1",
    )(x2d, g1, w1, w3, w2)


def _proj_kernel(h_ref, gm_ref, wrow_ref, wcol_ref, qt_ref, vt_ref, k_ref, u_ref, ga_ref, gs_ref):
    hn = _rms(h_ref[...], gm_ref[...]).astype(BF16)
    pr = _dot(hn, wrow_ref[...])
    c0 = 0
    for ref, is_gate in ((k_ref, False), (u_ref, False), (ga_ref, True), (gs_ref, True)):
        c1 = c0 + ref.shape[1]
        piece = pr[:, c0:c1]
        ref[...] = (_sigmoid(piece) if is_gate else piece).astype(BF16)
        c0 = c1
    pt = _dot_nt(wcol_ref[...], hn)
    nq = qt_ref.shape[0]
    qt_ref[...] = pt[:nq, :].astype(BF16)
    vt_ref[...] = pt[nq:, :].astype(BF16)


def _proj(h2d, gm, wrow, wcol, *, d_kv, d_u, tm, vmem):
    rows, d = h2d.shape
    tm = min(tm, rows)
    tok = lambda w: pl.BlockSpec((tm, w), lambda i: (i, 0))
    feat = lambda w: pl.BlockSpec((w, tm), lambda i: (0, i))
    return pl.pallas_call(
        _proj_kernel,
        grid=(rows // tm,),
        in_specs=[
            tok(d),
            _const_spec((1, d)),
            _const_spec(wrow.shape),
            _const_spec(wcol.shape),
        ],
        out_specs=[feat(d), feat(d_kv), tok(d_kv), tok(d_u), tok(d), tok(d)],
        out_shape=[
            jax.ShapeDtypeStruct((d, rows), BF16),
            jax.ShapeDtypeStruct((d_kv, rows), BF16),
            jax.ShapeDtypeStruct((rows, d_kv), BF16),
            jax.ShapeDtypeStruct((rows, d_u), BF16),
            jax.ShapeDtypeStruct((rows, d), BF16),
            jax.ShapeDtypeStruct((rows, d), BF16),
        ],
        compiler_params=pltpu.CompilerParams(
            dimension_semantics=("parallel",), vmem_limit_bytes=vmem),
        name="proj",
    )(h2d, gm, wrow, wcol)


def _ffn_out_kernel(h_ref, ag_ref, yg_ref, gs_ref, ga_ref, gb_ref, wout_ref, g2_ref,
                    w1_ref, w3_ref, w2_ref, gf_ref, o_ref):
    yg = yg_ref[...]
    ssm = _dot(yg, ga_ref[...]) * _sigmoid(_dot(yg, gb_ref[...])) * gs_ref[...].astype(F32)
    merged = (ag_ref[...].astype(F32) + ssm).astype(BF16)
    h = h_ref[...] + _dot(merged, wout_ref[...])
    h = h + 0.5 * _swiglu(_rms(h, g2_ref[...]).astype(BF16), w1_ref, w3_ref, w2_ref)
    o_ref[...] = _rms(h, gf_ref[...])


def _ffn_out(h1, ag, yg, gs, ga, gb, wout, g2, w1, w3, w2, gf, *, tm, vmem):
    rows, d = h1.shape
    dff = w1.shape[1]
    width = yg.shape[1]
    tm = min(tm, rows)
    row_spec = pl.BlockSpec((tm, d), lambda i: (i, 0))
    return pl.pallas_call(
        _ffn_out_kernel,
        grid=(rows // tm,),
        in_specs=[
            row_spec, row_spec, pl.BlockSpec((tm, width), lambda i: (i, 0)), row_spec,
            _const_spec((width, d)),
            _const_spec((width, d)),
            _const_spec((d, d)),
            _const_spec((1, d)),
            _const_spec((d, dff)),
            _const_spec((d, dff)),
            _const_spec((dff, d)),
            _const_spec((1, d)),
        ],
        out_specs=row_spec,
        out_shape=jax.ShapeDtypeStruct((rows, d), F32),
        compiler_params=pltpu.CompilerParams(
            dimension_semantics=("parallel",), vmem_limit_bytes=vmem),
        name="ffn_out",
    )(h1, ag, yg, gs, ga, gb, wout, g2, w1, w3, w2, gf)


def _attn_kernel(sink_ref, qt_ref, kc_ref, kp_ref, km_ref, vtc_ref, vtp_ref, vtm_ref, gate_ref, o_ref,
                 *, tq, n_kv):
    nsub = tq // WINDOW
    first_block = pl.program_id(1) * nsub
    lanes4 = Q_PER_KV * WINDOW
    kj = lax.broadcasted_iota(jnp.int32, (WINDOW, lanes4), 0)
    qi = lax.broadcasted_iota(jnp.int32, (WINDOW, lanes4), 1) % WINDOW
    cur_ok = kj <= qi
    cur_f = cur_ok.astype(F32)
    pair_w = 2 * HEAD_DIM

    def cols_of(j):
        return pl.ds(j * WINDOW, WINDOW)

    def scores(j, kh):
        if j == 0:
            k_prev = kp_ref[...]
        else:
            k_prev = kc_ref[cols_of(j - 1), :]
        k_all = jnp.concatenate([kc_ref[cols_of(j), :], k_prev, km_ref[...]], axis=0)
        qt4 = jnp.concatenate(
            [qt_ref[pl.ds((kh * Q_PER_KV + g) * HEAD_DIM, HEAD_DIM), cols_of(j)] for g in range(Q_PER_KV)],
            axis=1)
        zeros = jnp.zeros_like(qt4)
        w = jnp.concatenate([qt4, zeros] if kh % 2 == 0 else [zeros, qt4], axis=0)
        pair = kh // 2
        return _dot(k_all[:, pair * pair_w:(pair + 1) * pair_w], w)

    tasks = [(j, kh) for j in range(nsub) for kh in range(n_kv)]
    s_next = scores(*tasks[0])
    head_out = []
    for idx, (j, kh) in enumerate(tasks):
        s_all = s_next
        if idx + 1 < len(tasks):
            s_next = scores(*tasks[idx + 1])
        s_prev = s_all[WINDOW:2 * WINDOW]
        if j == 0:
            s_prev = s_prev + jnp.where(first_block > 0, 0.0, NEG_INF).astype(F32)
        s = jnp.where(cur_ok, s_all[:WINDOW], s_prev)
        s_meta = s_all[2 * WINDOW:]
        sink = jnp.concatenate(
            [jnp.full((1, WINDOW), sink_ref[kh * Q_PER_KV + g] * LOG2E, F32) for g in range(Q_PER_KV)],
            axis=1)
        m = jnp.maximum(jnp.maximum(jnp.max(s, axis=0, keepdims=True),
                                    jnp.max(s_meta, axis=0, keepdims=True)), sink)
        p = jnp.exp2(s - m)
        pm = jnp.exp2(s_meta - m)
        denom = (jnp.sum(p, axis=0, keepdims=True) + jnp.sum(pm, axis=0, keepdims=True)
                 + jnp.exp2(sink - m))
        p_cur = p * cur_f
        p_prev = p - p_cur
        hs = slice(kh * HEAD_DIM, (kh + 1) * HEAD_DIM)
        if j == 0:
            vt_prev = vtp_ref[hs, :]
        else:
            vt_prev = vtc_ref[hs, cols_of(j - 1)]
        vt_band = jnp.concatenate([vtc_ref[hs, cols_of(j)], vt_prev], axis=1)
        p_band = jnp.concatenate([p_cur.astype(BF16), p_prev.astype(BF16)], axis=0)
        ot = _dot(vt_band, p_band) + _dot(vtm_ref[hs, :], pm.astype(BF16))
        ot = ot * (1.0 / denom)
        for g in range(Q_PER_KV):
            head_out.append(ot[:, g * WINDOW:(g + 1) * WINDOW])
        if kh == n_kv - 1:
            attn = jnp.concatenate(head_out, axis=0).T
            o_ref[cols_of(j), :] = (attn * gate_ref[cols_of(j), :].astype(F32)).astype(BF16)
            head_out = []


def _attention(sinks, qt, k, vt, k_meta, vt_meta, gate, *, batch, seq, n_kv, tq, vmem):
    d, rows = qt.shape
    kvw = n_kv * HEAD_DIM
    tq = min(tq, seq)
    nsub = tq // WINDOW
    steps = seq // tq
    blocks = seq // WINDOW

    def prev_block(b, i):
        return jnp.maximum(b * blocks + i * nsub - 1, 0)

    return pl.pallas_call(
        functools.partial(_attn_kernel, tq=tq, n_kv=n_kv),
        grid=(batch, steps),
        in_specs=[
            pl.BlockSpec(memory_space=pltpu.SMEM),
            pl.BlockSpec((d, tq), lambda b, i: (0, b * steps + i)),
            pl.BlockSpec((tq, kvw), lambda b, i: (b * steps + i, 0)),
            pl.BlockSpec((WINDOW, kvw), lambda b, i: (prev_block(b, i), 0)),
            _const_spec((N_META, kvw)),
            pl.BlockSpec((kvw, tq), lambda b, i: (0, b * steps + i)),
            pl.BlockSpec((kvw, WINDOW), lambda b, i: (0, prev_block(b, i))),
            _const_spec((kvw, N_META)),
            pl.BlockSpec((tq, d), lambda b, i: (b * steps + i, 0)),
        ],
        out_specs=pl.BlockSpec((tq, d), lambda b, i: (b * steps + i, 0)),
        out_shape=jax.ShapeDtypeStruct((rows, d), BF16),
        compiler_params=pltpu.CompilerParams(
            dimension_semantics=("parallel", "parallel"), vmem_limit_bytes=vmem),
        name="attention",
    )(sinks, qt, k, k, k_meta, vt, vt, vt_meta, gate)


def _ssm_prep_kernel(ar_ref, ai_ref, ls_ref, br_ref, bi_ref, lr_ref, li_ref, bbr_ref, bbi_ref):
    ar = ar_ref[...]
    ai = ai_ref[...]
    step = jnp.exp(ls_ref[...])
    mag = jnp.exp(ar * step)
    ang = ai * step
    lam_re = mag * jnp.cos(ang)
    lam_im = mag * jnp.sin(ang)
    den = ar * ar + ai * ai
    nr = lam_re - 1.0
    ni = lam_im
    coef_re = (nr * ar + ni * ai) / den
    coef_im = (ni * ar - nr * ai) / den
    lr_ref[...] = lam_re
    li_ref[...] = lam_im
    br = br_ref[...]
    bi = bi_ref[...]
    cr = coef_re[:, None, :]
    ci = coef_im[:, None, :]
    bbr_ref[...] = cr * br - ci * bi
    bbi_ref[...] = cr * bi + ci * br


def _ssm_prep(a_re, a_im, log_step, b_re_t, b_im_t):
    g, n = a_re.shape
    c = b_re_t.shape[1]
    return pl.pallas_call(
        _ssm_prep_kernel,
        out_shape=[
            jax.ShapeDtypeStruct((g, n), F32),
            jax.ShapeDtypeStruct((g, n), F32),
            jax.ShapeDtypeStruct((g, c, n), F32),
            jax.ShapeDtypeStruct((g, c, n), F32),
        ],
        name="ssm_prep",
    )(a_re, a_im, log_step, b_re_t, b_im_t)


def _meta_state_kernel(u_ref, wb0_ref, wb1_ref, lr_ref, li_ref, x0_ref, *, half):
    u = u_ref[...]
    hw = u.shape[1] // 2
    bu0 = _dot(u[:, :hw], wb0_ref[...])
    bu1 = _dot(u[:, hw:], wb1_ref[...])
    bre = jnp.concatenate([bu0[:, :half], bu1[:, :half]], axis=-1)
    bim = jnp.concatenate([bu0[:, half:], bu1[:, half:]], axis=-1)
    lr = lr_ref[...]
    li = li_ref[...]
    xr = jnp.zeros_like(lr)
    xi = jnp.zeros_like(li)
    for t in range(u.shape[0]):
        nr = lr * xr - li * xi + bre[t:t + 1, :]
        ni = lr * xi + li * xr + bim[t:t + 1, :]
        xr, xi = nr, ni
    x0_ref[...] = jnp.concatenate([xr, xi], axis=-1)


def _meta_state(u_meta, wb0, wb1, lam_re_row, lam_im_row):
    nstate = lam_re_row.shape[1]
    return pl.pallas_call(
        functools.partial(_meta_state_kernel, half=nstate // 2),
        out_shape=jax.ShapeDtypeStruct((1, 2 * nstate), F32),
        name="meta_state",
    )(u_meta, wb0, wb1, lam_re_row, lam_im_row)


def _gelu_tanh(y):
    c = math.sqrt(2.0 / math.pi)
    return 0.5 * y * (1.0 + jnp.tanh(c * (y + 0.044715 * (y * y * y))))


def _ssm_kernel(u_ref, x0_ref, wb0_ref, wb1_ref, lr_ref, li_ref,
                wcr0_ref, wci0_ref, wcr1_ref, wci1_ref, dsk_ref,
                o_ref, bu_ref, xs_ref, up_ref, st_ref, *, tc, nstate, lane_chunk):
    nb = SSM_BATCH_ROWS
    half = nstate // 2
    width = u_ref.shape[2]
    hw = width // 2

    @pl.when(pl.program_id(1) == 0)
    def _():
        st_ref[...] = jnp.broadcast_to(x0_ref[...], st_ref.shape)

    ri = lax.broadcasted_iota(jnp.int32, (PERM_ROWS, PERM_ROWS), 0)
    ci = lax.broadcasted_iota(jnp.int32, (PERM_ROWS, PERM_ROWS), 1)
    src_of_row = (ri % nb) * V7X_BF16_ROWS + ri // nb
    perm = (ci == src_of_row).astype(BF16)
    dst_of_row = (ri % V7X_BF16_ROWS) * nb + ri // V7X_BF16_ROWS
    perm_t = (ci == dst_of_row).astype(BF16)

    nslice = SSM_OUT_SLICES
    ts = tc // nslice
    rs = nb * ts

    def project_in(s):
        rows = pl.ds(s * rs, rs)
        for k in range(ts // V7X_BF16_ROWS):
            t0 = s * ts + k * V7X_BF16_ROWS
            xk = jnp.concatenate(
                [u_ref[b, pl.ds(t0, V7X_BF16_ROWS), :] for b in range(nb)], axis=0)
            up_ref[pl.ds(s * rs + k * PERM_ROWS, PERM_ROWS), :] = _dot(perm, xk).astype(BF16)
        up = up_ref[rows, :]
        bu0 = _dot(up[:, :hw], wb0_ref[...])
        bu_ref[rows, 0:half] = bu0[:, :half]
        bu_ref[rows, nstate:nstate + half] = bu0[:, half:]
        bu1 = _dot(up[:, hw:], wb1_ref[...])
        bu_ref[rows, half:nstate] = bu1[:, :half]
        bu_ref[rows, nstate + half:] = bu1[:, half:]

    def scan(s):
        for c in range(nstate // lane_chunk):
            re_l = pl.ds(c * lane_chunk, lane_chunk)
            im_l = pl.ds(nstate + c * lane_chunk, lane_chunk)
            lr = jnp.broadcast_to(lr_ref[:, re_l], (nb, lane_chunk))
            li = jnp.broadcast_to(li_ref[:, re_l], (nb, lane_chunk))
            xr = st_ref[:, re_l]
            xi = st_ref[:, im_l]
            for t in range(s * ts, (s + 1) * ts):
                r = pl.ds(t * nb, nb)
                nr = lr * xr - li * xi + bu_ref[r, re_l]
                ni = lr * xi + li * xr + bu_ref[r, im_l]
                xs_ref[r, re_l] = nr
                xs_ref[r, im_l] = ni
                xr, xi = nr, ni
            st_ref[:, re_l] = xr
            st_ref[:, im_l] = xi

    def readout(s):
        rows = pl.ds(s * rs, rs)
        y0 = (_dot(xs_ref[rows, 0:half].astype(BF16), wcr0_ref[...])
              + _dot(xs_ref[rows, nstate:nstate + half].astype(BF16), wci0_ref[...]))
        y1 = (_dot(xs_ref[rows, half:nstate].astype(BF16), wcr1_ref[...])
              + _dot(xs_ref[rows, nstate + half:].astype(BF16), wci1_ref[...]))
        return jnp.concatenate([y0, y1], axis=-1) + dsk_ref[...] * up_ref[rows, :].astype(F32)

    def activate(s, y):
        yg = _gelu_tanh(y).astype(BF16)
        for k in range(ts // V7X_BF16_ROWS):
            yk = _dot(perm_t, yg[k * PERM_ROWS:(k + 1) * PERM_ROWS, :]).astype(BF16)
            t0 = s * ts + k * V7X_BF16_ROWS
            for b in range(nb):
                o_ref[b, pl.ds(t0, V7X_BF16_ROWS), :] = yk[b * V7X_BF16_ROWS:(b + 1) * V7X_BF16_ROWS, :]

    project_in(0)
    for s in range(nslice):
        if s + 1 < nslice:
            project_in(s + 1)
        scan(s)
        activate(s, readout(s))


def _ssm(u3, x0, wb0, wb1, lam_re_row, lam_im_row, wcr0, wci0, wcr1, wci1, dsk,
         *, tc, lane_chunk, vmem):
    batch, seq, width = u3.shape
    nstate = lam_re_row.shape[1]
    nb = SSM_BATCH_ROWS
    tc = min(tc, seq)
    consts = [x0, wb0, wb1, lam_re_row, lam_im_row, wcr0, wci0, wcr1, wci1, dsk]
    return pl.pallas_call(
        functools.partial(_ssm_kernel, tc=tc, nstate=nstate, lane_chunk=lane_chunk),
        grid=(batch // nb, seq // tc),
        in_specs=[
            pl.BlockSpec((nb, tc, width), lambda g, t: (g, t, 0)),
        ] + [_const_spec(c.shape) for c in consts],
        out_specs=pl.BlockSpec((nb, tc, width), lambda g, t: (g, t, 0)),
        out_shape=jax.ShapeDtypeStruct((batch, seq, width), BF16),
        scratch_shapes=[
            pltpu.VMEM((nb * tc, 2 * nstate), F32),
            pltpu.VMEM((nb * tc, 2 * nstate), F32),
            pltpu.VMEM((nb * tc, width), BF16),
            pltpu.VMEM((nb, 2 * nstate), F32),
        ],
        compiler_params=pltpu.CompilerParams(
            dimension_semantics=("parallel", "arbitrary"), vmem_limit_bytes=vmem),
        name="ssm",
    )(u3, *consts)


def _block_diag(blocks):
    g, r, c = blocks.shape
    tiled = jnp.tile(blocks.reshape(g * r, c), (1, g))
    row_group = jnp.arange(g * r, dtype=jnp.int32)[:, None] // r
    col_group = jnp.arange(g * c, dtype=jnp.int32)[None, :] // c
    return jnp.where(row_group == col_group, tiled, 0.0)


def kernel(x, meta_tokens, ffn1_norm, ffn1_w1, ffn1_w3, ffn1_w2, mix_norm, w_in, attn_sinks, ssm_a_re, ssm_a_im, ssm_log_step, ssm_b_re, ssm_b_im, ssm_c_re, ssm_c_im, ssm_d, ssm_glu_a, ssm_glu_b, w_out, ffn2_norm, ffn2_w1, ffn2_w3, ffn2_w2, final_norm):
    batch, seq, d = x.shape
    n_q = d // HEAD_DIM
    n_kv = n_q // Q_PER_KV
    kvw = n_kv * HEAD_DIM
    width = d // 2
    groups = width // SSM_GROUP
    nstate = groups * SSM_STATE
    half = nstate // 2
    hw = width // 2
    assert seq % WINDOW == 0 and batch % SSM_BATCH_ROWS == 0
    vmem = V7X_VMEM_BYTES - 8 * 1024 * 1024

    bf = lambda w: w.astype(BF16)
    row = lambda v: v.reshape(1, -1).astype(F32)

    lam_re, lam_im, bb_re_t, bb_im_t = _ssm_prep(
        ssm_a_re[0].astype(F32), ssm_a_im[0].astype(F32), ssm_log_step[0].astype(F32).reshape(groups, 1),
        jnp.swapaxes(ssm_b_re[0].astype(F32), 1, 2), jnp.swapaxes(ssm_b_im[0].astype(F32), 1, 2))
    bd_re = _block_diag(bb_re_t)
    bd_im = _block_diag(bb_im_t)
    wb0 = bf(jnp.concatenate([bd_re[:hw, :half], bd_im[:hw, :half]], axis=1))
    wb1 = bf(jnp.concatenate([bd_re[hw:, half:], bd_im[hw:, half:]], axis=1))
    cd_re = _block_diag(jnp.swapaxes(ssm_c_re[0].astype(F32), 1, 2))
    cd_im = _block_diag(jnp.swapaxes(-ssm_c_im[0].astype(F32), 1, 2))
    wcr0, wci0 = bf(cd_re[:half, :hw]), bf(cd_im[:half, :hw])
    wcr1, wci1 = bf(cd_re[half:, hw:]), bf(cd_im[half:, hw:])
    lam_re_row = lam_re.reshape(1, nstate)
    lam_im_row = lam_im.reshape(1, nstate)

    w1a, w3a, w2a = bf(ffn1_w1[0]), bf(ffn1_w3[0]), bf(ffn1_w2[0])
    g1, gm = row(ffn1_norm[0]), row(mix_norm[0])
    win = w_in[0]
    c_k, c_v, c_u = d, d + kvw, d + 2 * kvw
    wcol = jnp.concatenate([bf(win[:, :c_k] * (HEAD_DIM ** -0.5 * LOG2E)).T, bf(win[:, c_v:c_u]).T], axis=0)
    wrow = bf(jnp.concatenate([win[:, c_k:c_v], win[:, c_u:]], axis=1))
    ffn1 = functools.partial(_ffn1, g1=g1, w1=w1a, w3=w3a, w2=w2a, vmem=vmem)
    proj = functools.partial(_proj, gm=gm, wrow=wrow, wcol=wcol, d_kv=kvw, d_u=width, vmem=vmem)

    _, vt_meta, k_meta, u_meta, _, _ = proj(ffn1(meta_tokens.astype(F32), tm=N_META), tm=N_META)
    x0 = _meta_state(u_meta, wb0, wb1, lam_re_row, lam_im_row)

    rows = batch * seq
    h1 = ffn1(x.reshape(rows, d), tm=512)
    qt, vt, k, u, gate_a, gate_s = proj(h1, tm=1024)

    attn_g = _attention(attn_sinks[0].astype(F32), qt, k, vt, k_meta, vt_meta, gate_a,
                        batch=batch, seq=seq, n_kv=n_kv, tq=1024, vmem=vmem)
    yg = _ssm(u.reshape(batch, seq, width), x0, wb0, wb1, lam_re_row, lam_im_row,
              wcr0, wci0, wcr1, wci1, row(ssm_d[0]), tc=128, lane_chunk=512, vmem=vmem)

    out = _ffn_out(h1, attn_g, yg.reshape(rows, width), gate_s, bf(ssm_glu_a[0]), bf(ssm_glu_b[0]),
                   bf(w_out[0]), row(ffn2_norm[0]), bf(ffn2_w1[0]), bf(ffn2_w3[0]), bf(ffn2_w2[0]),
                   row(final_norm), tm=512, vmem=vmem)
    return out.reshape(batch, seq, d)
```

```python
import functools
import math

import jax
import jax.numpy as jnp
from jax import lax
from jax.experimental import pallas as pl
from jax.experimental.pallas import tpu as pltpu

F32 = jnp.float32
BF16 = jnp.bfloat16

N_META = 16
HEAD_DIM = 64
Q_PER_KV = 4
WINDOW = 128
SSM_GROUP = 16
SSM_STATE = 64
NORM_EPS = 1e-6
NEG_INF = -1e30
LOG2E = math.log2(math.e)

V7X_VMEM_BYTES = 64 * 1024 * 1024
V7X_SUBLANES = 8
V7X_BF16_ROWS = 16
SSM_BATCH_ROWS = V7X_SUBLANES
PERM_ROWS = V7X_BF16_ROWS * SSM_BATCH_ROWS
SSM_OUT_SLICES = 1

def _const_spec(shape):
    nd = len(shape)
    return pl.BlockSpec(shape, lambda *_: (0,) * nd, pipeline_mode=pl.Buffered(1))


def _dot(a, b):
    return jnp.dot(a, b, preferred_element_type=F32)


def _dot_nt(a, b):
    return lax.dot_general(a, b, (((1,), (1,)), ((), ())), preferred_element_type=F32)


def _rms(x, g):
    return x * lax.rsqrt(jnp.mean(x * x, axis=-1, keepdims=True) + NORM_EPS) * g


def _sigmoid(x):
    return 0.5 * jnp.tanh(0.5 * x) + 0.5


def _swiglu(xn_bf16, w1_ref, w3_ref, w2_ref):
    a = _dot(xn_bf16, w1_ref[...])
    b = _dot(xn_bf16, w3_ref[...])
    act = (a * _sigmoid(a) * b).astype(BF16)
    return _dot(act, w2_ref[...])


def _ffn1_kernel(x_ref, g1_ref, w1_ref, w3_ref, w2_ref, h_ref):
    x = x_ref[...]
    h_ref[...] = x + 0.5 * _swiglu(_rms(x, g1_ref[...]).astype(BF16), w1_ref, w3_ref, w2_ref)


def _ffn1(x2d, g1, w1, w3, w2, *, tm, vmem):
    rows, d = x2d.shape
    dff = w1.shape[1]
    tm = min(tm, rows)
    tok = pl.BlockSpec((tm, d), lambda i: (i, 0))
    return pl.pallas_call(
        _ffn1_kernel,
        grid=(rows // tm,),
        in_specs=[tok, _const_spec((1, d)), _const_spec((d, dff)), _const_spec((d, dff)),
                  _const_spec((dff, d))],
        out_specs=tok,
        out_shape=jax.ShapeDtypeStruct((rows, d), F32),
        compiler_params=pltpu.CompilerParams(
            dimension_semantics=("parallel",), vmem_limit_bytes=vmem),
        name="ffn1",
    )(x2d, g1, w1, w3, w2)


def _proj_kernel(h_ref, gm_ref, wrow_ref, wcol_ref, qt_ref, vt_ref, k_ref, u_ref, ga_ref, gs_ref):
    hn = _rms(h_ref[...], gm_ref[...]).astype(BF16)
    pr = _dot(hn, wrow_ref[...])
    c0 = 0
    for ref, is_gate in ((k_ref, False), (u_ref, False), (ga_ref, True), (gs_ref, True)):
        c1 = c0 + ref.shape[1]
        piece = pr[:, c0:c1]
        ref[...] = (_sigmoid(piece) if is_gate else piece).astype(BF16)
        c0 = c1
    pt = _dot_nt(wcol_ref[...], hn)
    nq = qt_ref.shape[0]
    qt_ref[...] = pt[:nq, :].astype(BF16)
    vt_ref[...] = pt[nq:, :].astype(BF16)


def _proj(h2d, gm, wrow, wcol, *, d_kv, d_u, tm, vmem):
    rows, d = h2d.shape
    tm = min(tm, rows)
    tok = lambda w: pl.BlockSpec((tm, w), lambda i: (i, 0))
    feat = lambda w: pl.BlockSpec((w, tm), lambda i: (0, i))
    return pl.pallas_call(
        _proj_kernel,
        grid=(rows // tm,),
        in_specs=[
            tok(d),
            _const_spec((1, d)),
            _const_spec(wrow.shape),
            _const_spec(wcol.shape),
        ],
        out_specs=[feat(d), feat(d_kv), tok(d_kv), tok(d_u), tok(d), tok(d)],
        out_shape=[
            jax.ShapeDtypeStruct((d, rows), BF16),
            jax.ShapeDtypeStruct((d_kv, rows), BF16),
            jax.ShapeDtypeStruct((rows, d_kv), BF16),
            jax.ShapeDtypeStruct((rows, d_u), BF16),
            jax.ShapeDtypeStruct((rows, d), BF16),
            jax.ShapeDtypeStruct((rows, d), BF16),
        ],
        compiler_params=pltpu.CompilerParams(
            dimension_semantics=("parallel",), vmem_limit_bytes=vmem),
        name="proj",
    )(h2d, gm, wrow, wcol)


def _ffn_out_kernel(h_ref, ag_ref, y_ref, gs_ref, ga_ref, gb_ref, wout_ref, g2_ref,
                    w1_ref, w3_ref, w2_ref, gf_ref, o_ref):
    yg = _gelu_tanh(y_ref[...].astype(F32)).astype(BF16)
    ssm = _dot(yg, ga_ref[...]) * _sigmoid(_dot(yg, gb_ref[...])) * gs_ref[...].astype(F32)
    merged = (ag_ref[...].astype(F32) + ssm).astype(BF16)
    h = h_ref[...] + _dot(merged, wout_ref[...])
    h = h + 0.5 * _swiglu(_rms(h, g2_ref[...]).astype(BF16), w1_ref, w3_ref, w2_ref)
    o_ref[...] = _rms(h, gf_ref[...])


def _ffn_out(h1, ag, yg, gs, ga, gb, wout, g2, w1, w3, w2, gf, *, tm, vmem):
    rows, d = h1.shape
    dff = w1.shape[1]
    width = yg.shape[1]
    tm = min(tm, rows)
    row_spec = pl.BlockSpec((tm, d), lambda i: (i, 0))
    return pl.pallas_call(
        _ffn_out_kernel,
        grid=(rows // tm,),
        in_specs=[
            row_spec, row_spec, pl.BlockSpec((tm, width), lambda i: (i, 0)), row_spec,
            _const_spec((width, d)),
            _const_spec((width, d)),
            _const_spec((d, d)),
            _const_spec((1, d)),
            _const_spec((d, dff)),
            _const_spec((d, dff)),
            _const_spec((dff, d)),
            _const_spec((1, d)),
        ],
        out_specs=row_spec,
        out_shape=jax.ShapeDtypeStruct((rows, d), F32),
        compiler_params=pltpu.CompilerParams(
            dimension_semantics=("parallel",), vmem_limit_bytes=vmem),
        name="ffn_out",
    )(h1, ag, yg, gs, ga, gb, wout, g2, w1, w3, w2, gf)


def _attn_kernel(sink_ref, qt_ref, kc_ref, kp_ref, km_ref, vtc_ref, vtp_ref, vtm_ref, gate_ref, o_ref,
                 *, tq, n_kv):
    nsub = tq // WINDOW
    first_block = pl.program_id(1) * nsub
    lanes4 = Q_PER_KV * WINDOW
    kj = lax.broadcasted_iota(jnp.int32, (WINDOW, lanes4), 0)
    qi = lax.broadcasted_iota(jnp.int32, (WINDOW, lanes4), 1) % WINDOW
    cur_ok = kj <= qi
    cur_f = cur_ok.astype(F32)
    pair_w = 2 * HEAD_DIM

    def cols_of(j):
        return pl.ds(j * WINDOW, WINDOW)

    def scores(j, kh):
        if j == 0:
            k_prev = kp_ref[...]
        else:
            k_prev = kc_ref[cols_of(j - 1), :]
        k_all = jnp.concatenate([kc_ref[cols_of(j), :], k_prev, km_ref[...]], axis=0)
        qt4 = jnp.concatenate(
            [qt_ref[pl.ds((kh * Q_PER_KV + g) * HEAD_DIM, HEAD_DIM), cols_of(j)] for g in range(Q_PER_KV)],
            axis=1)
        zeros = jnp.zeros_like(qt4)
        w = jnp.concatenate([qt4, zeros] if kh % 2 == 0 else [zeros, qt4], axis=0)
        pair = kh // 2
        return _dot(k_all[:, pair * pair_w:(pair + 1) * pair_w], w)

    tasks = [(j, kh) for j in range(nsub) for kh in range(n_kv)]
    s_next = scores(*tasks[0])
    head_out = []
    for idx, (j, kh) in enumerate(tasks):
        s_all = s_next
        if idx + 1 < len(tasks):
            s_next = scores(*tasks[idx + 1])
        s_prev = s_all[WINDOW:2 * WINDOW]
        if j == 0:
            s_prev = s_prev + jnp.where(first_block > 0, 0.0, NEG_INF).astype(F32)
        s = jnp.where(cur_ok, s_all[:WINDOW], s_prev)
        s_meta = s_all[2 * WINDOW:]
        sink = jnp.concatenate(
            [jnp.full((1, WINDOW), sink_ref[kh * Q_PER_KV + g] * LOG2E, F32) for g in range(Q_PER_KV)],
            axis=1)
        m = jnp.maximum(jnp.maximum(jnp.max(s, axis=0, keepdims=True),
                                    jnp.max(s_meta, axis=0, keepdims=True)), sink)
        p = jnp.exp2(s - m)
        pm = jnp.exp2(s_meta - m)
        denom = (jnp.sum(p, axis=0, keepdims=True) + jnp.sum(pm, axis=0, keepdims=True)
                 + jnp.exp2(sink - m))
        p_cur = p * cur_f
        p_prev = p - p_cur
        hs = slice(kh * HEAD_DIM, (kh + 1) * HEAD_DIM)
        if j == 0:
            vt_prev = vtp_ref[hs, :]
        else:
            vt_prev = vtc_ref[hs, cols_of(j - 1)]
        vt_band = jnp.concatenate([vtc_ref[hs, cols_of(j)], vt_prev], axis=1)
        p_band = jnp.concatenate([p_cur.astype(BF16), p_prev.astype(BF16)], axis=0)
        ot = _dot(vt_band, p_band) + _dot(vtm_ref[hs, :], pm.astype(BF16))
        ot = ot * (1.0 / denom)
        for g in range(Q_PER_KV):
            head_out.append(ot[:, g * WINDOW:(g + 1) * WINDOW])
        if kh == n_kv - 1:
            attn = jnp.concatenate(head_out, axis=0).T
            o_ref[cols_of(j), :] = (attn * gate_ref[cols_of(j), :].astype(F32)).astype(BF16)
            head_out = []


def _attention(sinks, qt, k, vt, k_meta, vt_meta, gate, *, batch, seq, n_kv, tq, vmem):
    d, rows = qt.shape
    kvw = n_kv * HEAD_DIM
    tq = min(tq, seq)
    nsub = tq // WINDOW
    steps = seq // tq
    blocks = seq // WINDOW

    def prev_block(b, i):
        return jnp.maximum(b * blocks + i * nsub - 1, 0)

    return pl.pallas_call(
        functools.partial(_attn_kernel, tq=tq, n_kv=n_kv),
        grid=(batch, steps),
        in_specs=[
            pl.BlockSpec(memory_space=pltpu.SMEM),
            pl.BlockSpec((d, tq), lambda b, i: (0, b * steps + i)),
            pl.BlockSpec((tq, kvw), lambda b, i: (b * steps + i, 0)),
            pl.BlockSpec((WINDOW, kvw), lambda b, i: (prev_block(b, i), 0)),
            _const_spec((N_META, kvw)),
            pl.BlockSpec((kvw, tq), lambda b, i: (0, b * steps + i)),
            pl.BlockSpec((kvw, WINDOW), lambda b, i: (0, prev_block(b, i))),
            _const_spec((kvw, N_META)),
            pl.BlockSpec((tq, d), lambda b, i: (b * steps + i, 0)),
        ],
        out_specs=pl.BlockSpec((tq, d), lambda b, i: (b * steps + i, 0)),
        out_shape=jax.ShapeDtypeStruct((rows, d), BF16),
        compiler_params=pltpu.CompilerParams(
            dimension_semantics=("parallel", "parallel"), vmem_limit_bytes=vmem),
        name="attention",
    )(sinks, qt, k, k, k_meta, vt, vt, vt_meta, gate)


def _ssm_prep_kernel(ar_ref, ai_ref, ls_ref, br_ref, bi_ref, lr_ref, li_ref, bbr_ref, bbi_ref):
    ar = ar_ref[...]
    ai = ai_ref[...]
    step = jnp.exp(ls_ref[...])
    mag = jnp.exp(ar * step)
    ang = ai * step
    lam_re = mag * jnp.cos(ang)
    lam_im = mag * jnp.sin(ang)
    den = ar * ar + ai * ai
    nr = lam_re - 1.0
    ni = lam_im
    coef_re = (nr * ar + ni * ai) / den
    coef_im = (ni * ar - nr * ai) / den
    lr_ref[...] = lam_re
    li_ref[...] = lam_im
    br = br_ref[...]
    bi = bi_ref[...]
    cr = coef_re[:, None, :]
    ci = coef_im[:, None, :]
    bbr_ref[...] = cr * br - ci * bi
    bbi_ref[...] = cr * bi + ci * br


def _ssm_prep(a_re, a_im, log_step, b_re_t, b_im_t):
    g, n = a_re.shape
    c = b_re_t.shape[1]
    return pl.pallas_call(
        _ssm_prep_kernel,
        out_shape=[
            jax.ShapeDtypeStruct((g, n), F32),
            jax.ShapeDtypeStruct((g, n), F32),
            jax.ShapeDtypeStruct((g, c, n), F32),
            jax.ShapeDtypeStruct((g, c, n), F32),
        ],
        name="ssm_prep",
    )(a_re, a_im, log_step, b_re_t, b_im_t)


def _meta_state_kernel(u_ref, wb0_ref, wb1_ref, lr_ref, li_ref, x0_ref, *, half):
    u = u_ref[...]
    hw = u.shape[1] // 2
    bu0 = _dot(u[:, :hw], wb0_ref[...])
    bu1 = _dot(u[:, hw:], wb1_ref[...])
    bre = jnp.concatenate([bu0[:, :half], bu1[:, :half]], axis=-1)
    bim = jnp.concatenate([bu0[:, half:], bu1[:, half:]], axis=-1)
    lr = lr_ref[...]
    li = li_ref[...]
    xr = jnp.zeros_like(lr)
    xi = jnp.zeros_like(li)
    for t in range(u.shape[0]):
        nr = lr * xr - li * xi + bre[t:t + 1, :]
        ni = lr * xi + li * xr + bim[t:t + 1, :]
        xr, xi = nr, ni
    x0_ref[...] = jnp.concatenate([xr, xi], axis=-1)


def _meta_state(u_meta, wb0, wb1, lam_re_row, lam_im_row):
    nstate = lam_re_row.shape[1]
    return pl.pallas_call(
        functools.partial(_meta_state_kernel, half=nstate // 2),
        out_shape=jax.ShapeDtypeStruct((1, 2 * nstate), F32),
        name="meta_state",
    )(u_meta, wb0, wb1, lam_re_row, lam_im_row)


def _gelu_tanh(y):
    c = math.sqrt(2.0 / math.pi)
    return 0.5 * y * (1.0 + jnp.tanh(c * (y + 0.044715 * (y * y * y))))


def _ssm_kernel(u_ref, x0_ref, wb0_ref, wb1_ref, lr_ref, li_ref,
                wcr0_ref, wci0_ref, wcr1_ref, wci1_ref, dsk_ref,
                o_ref, bu_ref, xs_ref, up_ref, st_ref, *, tc, nstate, lane_chunk):
    nb = SSM_BATCH_ROWS
    half = nstate // 2
    width = u_ref.shape[2]
    hw = width // 2

    @pl.when(pl.program_id(1) == 0)
    def _():
        st_ref[...] = jnp.broadcast_to(x0_ref[...], st_ref.shape)

    ri = lax.broadcasted_iota(jnp.int32, (PERM_ROWS, PERM_ROWS), 0)
    ci = lax.broadcasted_iota(jnp.int32, (PERM_ROWS, PERM_ROWS), 1)
    src_of_row = (ri % nb) * V7X_BF16_ROWS + ri // nb
    perm = (ci == src_of_row).astype(BF16)
    dst_of_row = (ri % V7X_BF16_ROWS) * nb + ri // V7X_BF16_ROWS
    perm_t = (ci == dst_of_row).astype(BF16)

    nslice = SSM_OUT_SLICES
    ts = tc // nslice
    rs = nb * ts

    def project_in(s):
        rows = pl.ds(s * rs, rs)
        for k in range(ts // V7X_BF16_ROWS):
            t0 = s * ts + k * V7X_BF16_ROWS
            xk = jnp.concatenate(
                [u_ref[b, pl.ds(t0, V7X_BF16_ROWS), :] for b in range(nb)], axis=0)
            up_ref[pl.ds(s * rs + k * PERM_ROWS, PERM_ROWS), :] = _dot(perm, xk).astype(BF16)
        up = up_ref[rows, :]
        bu0 = _dot(up[:, :hw], wb0_ref[...])
        bu_ref[rows, 0:half] = bu0[:, :half]
        bu_ref[rows, nstate:nstate + half] = bu0[:, half:]
        bu1 = _dot(up[:, hw:], wb1_ref[...])
        bu_ref[rows, half:nstate] = bu1[:, :half]
        bu_ref[rows, nstate + half:] = bu1[:, half:]

    def scan(s):
        for c in range(nstate // lane_chunk):
            re_l = pl.ds(c * lane_chunk, lane_chunk)
            im_l = pl.ds(nstate + c * lane_chunk, lane_chunk)
            lr = jnp.broadcast_to(lr_ref[:, re_l], (nb, lane_chunk))
            li = jnp.broadcast_to(li_ref[:, re_l], (nb, lane_chunk))
            xr = st_ref[:, re_l]
            xi = st_ref[:, im_l]
            for t in range(s * ts, (s + 1) * ts):
                r = pl.ds(t * nb, nb)
                nr = lr * xr - li * xi + bu_ref[r, re_l]
                ni = lr * xi + li * xr + bu_ref[r, im_l]
                xs_ref[r, re_l] = nr
                xs_ref[r, im_l] = ni
                xr, xi = nr, ni
            st_ref[:, re_l] = xr
            st_ref[:, im_l] = xi

    def readout(s):
        rows = pl.ds(s * rs, rs)
        y0 = (_dot(xs_ref[rows, 0:half].astype(BF16), wcr0_ref[...])
              + _dot(xs_ref[rows, nstate:nstate + half].astype(BF16), wci0_ref[...]))
        y1 = (_dot(xs_ref[rows, half:nstate].astype(BF16), wcr1_ref[...])
              + _dot(xs_ref[rows, nstate + half:].astype(BF16), wci1_ref[...]))
        return jnp.concatenate([y0, y1], axis=-1) + dsk_ref[...] * up_ref[rows, :].astype(F32)

    def activate(s, y):
        yb = y.astype(BF16)
        for k in range(ts // V7X_BF16_ROWS):
            yk = _dot(perm_t, yb[k * PERM_ROWS:(k + 1) * PERM_ROWS, :]).astype(BF16)
            t0 = s * ts + k * V7X_BF16_ROWS
            for b in range(nb):
                o_ref[b, pl.ds(t0, V7X_BF16_ROWS), :] = yk[b * V7X_BF16_ROWS:(b + 1) * V7X_BF16_ROWS, :]

    project_in(0)
    for s in range(nslice):
        if s + 1 < nslice:
            project_in(s + 1)
        scan(s)
        activate(s, readout(s))


def _ssm(u3, x0, wb0, wb1, lam_re_row, lam_im_row, wcr0, wci0, wcr1, wci1, dsk,
         *, tc, lane_chunk, vmem):
    batch, seq, width = u3.shape
    nstate = lam_re_row.shape[1]
    nb = SSM_BATCH_ROWS
    tc = min(tc, seq)
    consts = [x0, wb0, wb1, lam_re_row, lam_im_row, wcr0, wci0, wcr1, wci1, dsk]
    return pl.pallas_call(
        functools.partial(_ssm_kernel, tc=tc, nstate=nstate, lane_chunk=lane_chunk),
        grid=(batch // nb, seq // tc),
        in_specs=[
            pl.BlockSpec((nb, tc, width), lambda g, t: (g, t, 0)),
        ] + [_const_spec(c.shape) for c in consts],
        out_specs=pl.BlockSpec((nb, tc, width), lambda g, t: (g, t, 0)),
        out_shape=jax.ShapeDtypeStruct((batch, seq, width), BF16),
        scratch_shapes=[
            pltpu.VMEM((nb * tc, 2 * nstate), F32),
            pltpu.VMEM((nb * tc, 2 * nstate), F32),
            pltpu.VMEM((nb * tc, width), BF16),
            pltpu.VMEM((nb, 2 * nstate), F32),
        ],
        compiler_params=pltpu.CompilerParams(
            dimension_semantics=("parallel", "arbitrary"), vmem_limit_bytes=vmem),
        name="ssm",
    )(u3, *consts)


def _block_diag(blocks):
    g, r, c = blocks.shape
    tiled = jnp.tile(blocks.reshape(g * r, c), (1, g))
    row_group = jnp.arange(g * r, dtype=jnp.int32)[:, None] // r
    col_group = jnp.arange(g * c, dtype=jnp.int32)[None, :] // c
    return jnp.where(row_group == col_group, tiled, 0.0)


def kernel(x, meta_tokens, ffn1_norm, ffn1_w1, ffn1_w3, ffn1_w2, mix_norm, w_in, attn_sinks, ssm_a_re, ssm_a_im, ssm_log_step, ssm_b_re, ssm_b_im, ssm_c_re, ssm_c_im, ssm_d, ssm_glu_a, ssm_glu_b, w_out, ffn2_norm, ffn2_w1, ffn2_w3, ffn2_w2, final_norm):
    batch, seq, d = x.shape
    n_q = d // HEAD_DIM
    n_kv = n_q // Q_PER_KV
    kvw = n_kv * HEAD_DIM
    width = d // 2
    groups = width // SSM_GROUP
    nstate = groups * SSM_STATE
    half = nstate // 2
    hw = width // 2
    assert seq % WINDOW == 0 and batch % SSM_BATCH_ROWS == 0
    vmem = V7X_VMEM_BYTES - 8 * 1024 * 1024

    bf = lambda w: w.astype(BF16)
    row = lambda v: v.reshape(1, -1).astype(F32)

    lam_re, lam_im, bb_re_t, bb_im_t = _ssm_prep(
        ssm_a_re[0].astype(F32), ssm_a_im[0].astype(F32), ssm_log_step[0].astype(F32).reshape(groups, 1),
        jnp.swapaxes(ssm_b_re[0].astype(F32), 1, 2), jnp.swapaxes(ssm_b_im[0].astype(F32), 1, 2))
    bd_re = _block_diag(bb_re_t)
    bd_im = _block_diag(bb_im_t)
    wb0 = bf(jnp.concatenate([bd_re[:hw, :half], bd_im[:hw, :half]], axis=1))
    wb1 = bf(jnp.concatenate([bd_re[hw:, half:], bd_im[hw:, half:]], axis=1))
    cd_re = _block_diag(jnp.swapaxes(ssm_c_re[0].astype(F32), 1, 2))
    cd_im = _block_diag(jnp.swapaxes(-ssm_c_im[0].astype(F32), 1, 2))
    wcr0, wci0 = bf(cd_re[:half, :hw]), bf(cd_im[:half, :hw])
    wcr1, wci1 = bf(cd_re[half:, hw:]), bf(cd_im[half:, hw:])
    lam_re_row = lam_re.reshape(1, nstate)
    lam_im_row = lam_im.reshape(1, nstate)

    w1a, w3a, w2a = bf(ffn1_w1[0]), bf(ffn1_w3[0]), bf(ffn1_w2[0])
    g1, gm = row(ffn1_norm[0]), row(mix_norm[0])
    win = w_in[0]
    c_k, c_v, c_u = d, d + kvw, d + 2 * kvw
    wcol = jnp.concatenate([bf(win[:, :c_k] * (HEAD_DIM ** -0.5 * LOG2E)).T, bf(win[:, c_v:c_u]).T], axis=0)
    wrow = bf(jnp.concatenate([win[:, c_k:c_v], win[:, c_u:]], axis=1))
    ffn1 = functools.partial(_ffn1, g1=g1, w1=w1a, w3=w3a, w2=w2a, vmem=vmem)
    proj = functools.partial(_proj, gm=gm, wrow=wrow, wcol=wcol, d_kv=kvw, d_u=width, vmem=vmem)

    _, vt_meta, k_meta, u_meta, _, _ = proj(ffn1(meta_tokens.astype(F32), tm=N_META), tm=N_META)
    x0 = _meta_state(u_meta, wb0, wb1, lam_re_row, lam_im_row)

    rows = batch * seq
    h1 = ffn1(x.reshape(rows, d), tm=512)
    qt, vt, k, u, gate_a, gate_s = proj(h1, tm=1024)

    attn_g = _attention(attn_sinks[0].astype(F32), qt, k, vt, k_meta, vt_meta, gate_a,
                        batch=batch, seq=seq, n_kv=n_kv, tq=1024, vmem=vmem)
    yg = _ssm(u.reshape(batch, seq, width), x0, wb0, wb1, lam_re_row, lam_im_row,
              wcr0, wci0, wcr1, wci1, row(ssm_d[0]), tc=128, lane_chunk=512, vmem=vmem)

    out = _ffn_out(h1, attn_g, yg.reshape(rows, width), gate_s, bf(ssm_glu_a[0]), bf(ssm_glu_b[0]),
                   bf(w_out[0]), row(ffn2_norm[0]), bf(ffn2_w1[0]), bf(ffn2_w3[0]), bf(ffn2_w2[0]),
                   row(final_norm), tm=512, vmem=vmem)
    return out.reshape(batch, seq, d)
```
